```python
import math
import jax, jax.numpy as jnp
from jax import lax
import numpy as np

D_MODEL = 1024
BATCH = 4
SEQ = 8192
DEPTH = 1
DEC_BATCH = 8
DEC_SEQ = 4096
PAST_LEN = 128

HEAD_DIM = 64
N_ATTN_HEADS = 12
D_ATTN = N_ATTN_HEADS * HEAD_DIM
N_FOURIER_GROUPS = 4
FOURIER_GROUP = 64
D_FOURIER = N_FOURIER_GROUPS * FOURIER_GROUP
D_MIX = D_ATTN + D_FOURIER
D_IN_PROJ = 4 * D_ATTN + 2 * D_FOURIER
PATTERNS = ((128, 1), (512, 4), (2048, 16))
BLOCK = 64
N_BUCKETS = 32
MAX_DISTANCE = 1024
RMS_EPS = 1e-6
NEG = -1e30

kernel_name = "hybrid_dilated_attn_fnet_encoder"


def rmsnorm(x, g):
    xf = x.astype(jnp.float32)
    y = xf * lax.rsqrt(jnp.mean(xf * xf, axis=-1, keepdims=True) + RMS_EPS) * g.astype(jnp.float32)
    return y.astype(x.dtype)


def t5_bucket(rel):
    half = N_BUCKETS // 2
    max_exact = half // 2
    ret = jnp.where(rel > 0, half, 0)
    n = jnp.abs(rel)
    nf = jnp.maximum(n, 1).astype(jnp.float32)
    large = max_exact + (jnp.log(nf / max_exact) / math.log(MAX_DISTANCE / max_exact)
                         * (half - max_exact)).astype(jnp.int32)
    large = jnp.minimum(large, half - 1)
    return ret + jnp.where(n < max_exact, n, large)


def dilated_window_attention(q, k, v, rel_bias, window, dil):
    B, S, H, Dh = q.shape
    half = window // (2 * dil)
    L = S // dil
    nb = -(-L // BLOCK)
    Lp = nb * BLOCK

    def to_residue(t):
        return t.reshape(B, L, dil, H, Dh).transpose(0, 2, 1, 3, 4)

    qr, kr, vr = to_residue(q), to_residue(k), to_residue(v)
    qb = jnp.pad(qr, ((0, 0), (0, 0), (0, Lp - L), (0, 0), (0, 0))).reshape(B, dil, nb, BLOCK, H, Dh)

    def key_blocks(t):
        tp = jnp.pad(t, ((0, 0), (0, 0), (BLOCK, Lp - L + BLOCK), (0, 0), (0, 0)))
        tp = tp.reshape(B, dil, nb + 2, BLOCK, H, Dh)
        return jnp.concatenate([tp[:, :, 0:nb], tp[:, :, 1:nb + 1], tp[:, :, 2:nb + 2]], axis=3)

    kb, vb = key_blocks(kr), key_blocks(vr)

    a = jnp.arange(BLOCK, dtype=jnp.int32)[:, None]
    c = jnp.arange(3 * BLOCK, dtype=jnp.int32)[None, :]
    rel = c - BLOCK - a
    band = jnp.abs(rel) <= half
    key_pos = jnp.arange(nb, dtype=jnp.int32)[:, None] * BLOCK + jnp.arange(3 * BLOCK, dtype=jnp.int32)[None, :] - BLOCK
    in_range = (key_pos >= 0) & (key_pos < L)
    mask = band[None] & in_range[:, None, :]
    bias = rel_bias[t5_bucket(rel * dil)].astype(jnp.float32).transpose(2, 0, 1)

    scale = 1.0 / math.sqrt(Dh)
    s = jnp.einsum('brnqhd,brnkhd->brnhqk', qb, kb, preferred_element_type=jnp.float32) * scale + bias
    s = jnp.where(mask[:, None], s, NEG)
    lse = jax.nn.logsumexp(s, axis=-1)
    p = jnp.exp(s - lse[..., None])
    o = jnp.einsum('brnhqk,brnkhd->brnqhd', p.astype(vb.dtype), vb, preferred_element_type=jnp.float32)
    o = o.reshape(B, dil, Lp, H, Dh)[:, :, :L].transpose(0, 2, 1, 3, 4).reshape(B, S, H, Dh)
    lse = lse.transpose(0, 1, 2, 4, 3).reshape(B, dil, Lp, H)[:, :, :L].transpose(0, 2, 1, 3).reshape(B, S, H)
    return o, lse


def encoder_layer(x, norm_g, w_in, q_g, k_g, rel_bias, w_four, w_out):
    B, S, _ = x.shape
    h = rmsnorm(x, norm_g)
    proj = h @ w_in
    q, k, v, g_attn, u, g_four = jnp.split(
        proj, [D_ATTN, 2 * D_ATTN, 3 * D_ATTN, 4 * D_ATTN, 4 * D_ATTN + D_FOURIER], axis=-1)

    q = rmsnorm(q.reshape(B, S, N_ATTN_HEADS, HEAD_DIM), q_g)
    k = rmsnorm(k.reshape(B, S, N_ATTN_HEADS, HEAD_DIM), k_g)
    v = v.reshape(B, S, N_ATTN_HEADS, HEAD_DIM)
    outs, lses = [], []
    for window, dil in PATTERNS:
        o, l = dilated_window_attention(q, k, v, rel_bias, window, dil)
        outs.append(o)
        lses.append(l)
    wts = jax.nn.softmax(jnp.stack(lses), axis=0)
    attn = jnp.einsum('pbsh,pbshd->bshd', wts, jnp.stack(outs))
    attn = attn.reshape(B, S, D_ATTN).astype(x.dtype) * jax.nn.silu(g_attn)

    uf = u.astype(jnp.float32).reshape(B, S, N_FOURIER_GROUPS, FOURIER_GROUP)
    mixed = jnp.fft.fftn(uf, axes=(1, 3), norm="ortho").real
    four = jnp.einsum('bsgc,gce->bsge', mixed.astype(x.dtype), w_four).reshape(B, S, D_FOURIER)
    four = four * jax.nn.silu(g_four)

    y = jnp.concatenate([attn, four], axis=-1) @ w_out
    return x + y


def setup_inputs(seed: int = 0) -> dict:
    key = jax.random.key(seed)
    ks = jax.random.split(key, 10)
    f32 = jnp.float32
    x_prompt = jax.random.normal(ks[0], (BATCH, SEQ, D_MODEL), f32)
    x_sample = jax.random.normal(ks[1], (DEC_BATCH, DEC_SEQ, D_MODEL), f32)
    norm_g = 1.0 + 0.01 * jax.random.normal(ks[2], (DEPTH, D_MODEL), f32)
    w_in = jax.random.normal(ks[3], (DEPTH, D_MODEL, D_IN_PROJ), f32) * D_MODEL ** -0.5
    q_norm_g = 1.0 + 0.01 * jax.random.normal(ks[4], (DEPTH, HEAD_DIM), f32)
    k_norm_g = 1.0 + 0.01 * jax.random.normal(ks[5], (DEPTH, HEAD_DIM), f32)
    rel_bias = 0.1 * jax.random.normal(ks[6], (N_BUCKETS, N_ATTN_HEADS), f32)
    w_four = jax.random.normal(ks[7], (DEPTH, N_FOURIER_GROUPS, FOURIER_GROUP, FOURIER_GROUP), f32) * FOURIER_GROUP ** -0.5
    w_out = jax.random.normal(ks[8], (DEPTH, D_MIX, D_MODEL), f32) * D_MIX ** -0.5
    return {"x_prompt": x_prompt, "x_sample": x_sample, "norm_g": norm_g, "w_in": w_in,
            "q_norm_g": q_norm_g, "k_norm_g": k_norm_g, "rel_bias": rel_bias,
            "w_four": w_four, "w_out": w_out}


def reference(x_prompt, x_sample, norm_g, w_in, q_norm_g, k_norm_g, rel_bias, w_four, w_out):
    y_prompt = x_prompt
    y_sample = x_sample
    for layer in range(DEPTH):
        y_prompt = encoder_layer(y_prompt, norm_g[layer], w_in[layer], q_norm_g[layer], k_norm_g[layer],
                                 rel_bias, w_four[layer], w_out[layer])
        y_sample = encoder_layer(y_sample, norm_g[layer], w_in[layer], q_norm_g[layer], k_norm_g[layer],
                                 rel_bias, w_four[layer], w_out[layer])
    return (y_prompt, y_sample)
```

```python
import functools
import math

import numpy as np
import jax
import jax.numpy as jnp
from jax import lax
from jax.experimental import pallas as pl
from jax.experimental.pallas import tpu as pltpu

D_MODEL = 1024
HEAD_DIM = 64
N_HEADS = 12
N_PAIRS = N_HEADS // 2
D_ATTN = N_HEADS * HEAD_DIM
N_GROUPS = 4
GROUP = 64
D_FOURIER = N_GROUPS * GROUP
N_BUCKETS = 32
MAX_DISTANCE = 1024
RMS_EPS = 1e-6
NEG = -1e30
RES = 16
HALF = 64
TQ = 128
TK = 256
LANES = 128
VMEM_LIMIT = 52 * 1024 * 1024

C_Q, C_K, C_V, C_GA, C_GF, C_P, C_QQ, C_END = 0, 768, 1536, 2304, 3072, 3328, 3584, 3840

f32 = jnp.float32
bf16 = jnp.bfloat16


def _t5_bucket_np(rel):
    half = N_BUCKETS // 2
    max_exact = half // 2
    ret = np.where(rel > 0, half, 0)
    n = np.abs(rel)
    nf = np.maximum(n, 1).astype(np.float32)
    large = max_exact + (np.log(nf / np.float32(max_exact)) / np.float32(math.log(MAX_DISTANCE / max_exact))
                         * np.float32(half - max_exact)).astype(np.int32)
    large = np.minimum(large, half - 1)
    return ret + np.where(n < max_exact, n, large)


def _bucket_tiles():
    tiles = np.zeros((3, 3, TQ, TK), np.int32)
    i = np.arange(TQ)[:, None]
    c = np.arange(TK)[None, :]
    for var in range(3):
        rel = c - 64 * var - (16 * (i % 8) + i // 8)
        tiles[0, var] = np.where(np.abs(rel) <= HALF, _t5_bucket_np(rel * 1), -1)
        rel = 4 * ((c % 64) - (i % 32) - 16 * var) + (c // 64 - i // 32)
        tiles[1, var] = np.where(np.abs(rel) <= HALF, _t5_bucket_np(rel * 4), -1)
        rel = c - 64 * var - i
        tiles[2, var] = np.where(np.abs(rel) <= HALF, _t5_bucket_np(rel * 16), -1)
    return tiles.reshape(9, TQ, TK)


def _dft_tables(seq):
    m = seq // RES
    s1 = np.arange(m)[:, None]
    k2 = np.arange(RES)[None, :]
    ang = 2.0 * np.pi * ((s1 * k2) % seq) / seq
    tw = np.zeros((m, LANES), np.float32)
    tw[:, :RES] = np.cos(ang)
    tw[:, RES:2 * RES] = np.sin(ang)
    k1 = np.arange(m)[:, None]
    ang = 2.0 * np.pi * ((k1 * np.arange(m)[None, :]) % m) / m
    scale = 1.0 / math.sqrt(seq)
    dmat = np.concatenate([np.cos(ang) * scale, np.sin(ang) * scale], axis=1)
    return tw, dmat.astype(np.float32)


def _channel_dft_blockdiag():
    c = np.arange(GROUP)
    ang = 2.0 * np.pi * ((c[:, None] * c[None, :]) % GROUP) / GROUP
    eye = np.eye(N_GROUPS)
    scale = 1.0 / math.sqrt(GROUP)
    return (np.kron(eye, np.cos(ang) * scale).astype(np.float32),
            np.kron(eye, np.sin(ang) * scale).astype(np.float32))


def _bias_kernel(rb_ref, idx_ref, out_ref):
    idx = idx_ref[0]
    pair = pl.program_id(1)
    for hh in range(2):
        head = 2 * pair + hh
        t = jnp.zeros((TQ, TK), f32)
        for b in range(N_BUCKETS):
            t = jnp.where(idx == b, rb_ref[b, head], t)
        t = jnp.where(idx < 0, NEG, t)
        out_ref[0, 0, hh * TQ:(hh + 1) * TQ, :] = t


def _bias_tiles(rel_bias):
    idx = jnp.asarray(_bucket_tiles())
    return pl.pallas_call(
        _bias_kernel,
        grid=(9, N_PAIRS),
        in_specs=[pl.BlockSpec(memory_space=pltpu.SMEM),
                  pl.BlockSpec((1, TQ, TK), lambda p, j: (p, 0, 0))],
        out_specs=pl.BlockSpec((1, 1, 2 * TQ, TK), lambda p, j: (p, j, 0, 0)),
        out_shape=jax.ShapeDtypeStruct((9, N_PAIRS, 2 * TQ, TK), f32),
        name="bias_tiles",
    )(rel_bias, idx)


def _fold_kernel(wu_ref, ww_ref, cbd_ref, sbd_ref, wp_ref, wq_ref):
    hi = lax.Precision.HIGHEST
    gp = jnp.dot(cbd_ref[...], ww_ref[...], precision=hi, preferred_element_type=f32)
    gq = jnp.dot(sbd_ref[...], ww_ref[...], precision=hi, preferred_element_type=f32)
    wp_ref[...] = jnp.dot(wu_ref[...], gp, precision=hi, preferred_element_type=f32)
    wq_ref[...] = jnp.dot(wu_ref[...], gq, precision=hi, preferred_element_type=f32)


def _fold_fourier_weights(w_u, w_four):
    cbd, sbd = _channel_dft_blockdiag()
    blockmask = jnp.asarray(np.kron(np.eye(N_GROUPS), np.ones((GROUP, GROUP))).astype(np.float32))
    wwide = jnp.tile(w_four.reshape(D_FOURIER, GROUP), (1, N_GROUPS)) * blockmask
    return pl.pallas_call(
        _fold_kernel,
        out_shape=(jax.ShapeDtypeStruct((D_MODEL, D_FOURIER), f32),
                   jax.ShapeDtypeStruct((D_MODEL, D_FOURIER), f32)),
        name="fold_fourier",
    )(w_u, wwide, jnp.asarray(cbd), jnp.asarray(sbd))


def _silu(x):
    return x * (1.0 / (1.0 + jnp.exp(-x)))


def _inproj_kernel(x_ref, g_ref, w_ref, qg_ref, kg_ref, bd_ref,
                   q16_ref, k16_ref, v16_ref, ga16_ref, gf16_ref, kn_ref, vn_ref, p_ref, qq_ref,
                   scr_ref, *, tmm):
    xs = x_ref[0]
    ms = jnp.mean(xs * xs, axis=-1, keepdims=True)
    h = (xs * lax.rsqrt(ms + RMS_EPS) * g_ref[...]).astype(bf16)

    def proj(lo, hi):
        return jnp.dot(h, w_ref[:, lo:hi], preferred_element_type=f32)

    def head_norm(t, gain):
        t2 = (t * t).astype(bf16)
        msq = jnp.concatenate(
            [jnp.dot(t2[:, c * 256:(c + 1) * 256], bd_ref[...], preferred_element_type=f32)
             for c in range(D_ATTN // 256)], axis=-1)
        return t * lax.rsqrt(msq + RMS_EPS) * gain

    def residue_rows(slab, val):
        scr_ref[slab] = val
        return [scr_ref[slab, pl.ds(r, tmm, stride=RES), :] for r in range(RES)]

    def put_res(ref, val, first_slab):
        for j in range(N_PAIRS):
            rows = residue_rows(first_slab + j, val[:, j * LANES:(j + 1) * LANES])
            for r in range(RES):
                ref[0, j, r] = rows[r].astype(bf16)

    def put_nat(ref, val):
        for j in range(N_PAIRS):
            ref[0, j] = val[:, j * LANES:(j + 1) * LANES].astype(bf16)

    qn = head_norm(proj(C_Q, C_K), qg_ref[...])
    put_res(q16_ref, qn, 0)
    kn = head_norm(proj(C_K, C_V), kg_ref[...])
    put_res(k16_ref, kn, 6)
    put_nat(kn_ref, kn)
    v = proj(C_V, C_GA)
    put_res(v16_ref, v, 12)
    put_nat(vn_ref, v)
    put_res(ga16_ref, _silu(proj(C_GA, C_GF)), 18)
    gf = _silu(proj(C_GF, C_P))
    for c in range(D_FOURIER // LANES):
        rows = residue_rows(24 + c, gf[:, c * LANES:(c + 1) * LANES])
        for r in range(RES):
            gf16_ref[0, r, :, c * LANES:(c + 1) * LANES] = rows[r].astype(bf16)
    p_ref[0] = proj(C_P, C_QQ).astype(bf16)
    qq_ref[0] = proj(C_QQ, C_END).astype(bf16)


def _inproj(x, norm_g, w_ext, qg, kg, bd, tmm=32):
    B, S, _ = x.shape
    M = S // RES
    tm = RES * tmm
    res_shape = jax.ShapeDtypeStruct((B, N_PAIRS, RES, M, LANES), bf16)
    nat_shape = jax.ShapeDtypeStruct((B, N_PAIRS, S, LANES), bf16)
    four_shape = jax.ShapeDtypeStruct((B, S, D_FOURIER), bf16)
    res_spec = pl.BlockSpec((1, N_PAIRS, RES, tmm, LANES), lambda b, i: (b, 0, 0, i, 0))
    nat_spec = pl.BlockSpec((1, N_PAIRS, tm, LANES), lambda b, i: (b, 0, i, 0))
    four_spec = pl.BlockSpec((1, tm, D_FOURIER), lambda b, i: (b, i, 0))
    const = lambda shape: pl.BlockSpec(shape, lambda b, i: (0,) * len(shape))
    n_slabs = (4 * D_ATTN + D_FOURIER) // LANES
    return pl.pallas_call(
        functools.partial(_inproj_kernel, tmm=tmm),
        grid=(B, M // tmm),
        in_specs=[pl.BlockSpec((1, tm, D_MODEL), lambda b, i: (b, i, 0)),
                  const((1, D_MODEL)), const((D_MODEL, C_END)),
                  const((1, D_ATTN)), const((1, D_ATTN)), const((256, 256))],
        out_specs=[res_spec, res_spec, res_spec, res_spec,
                   pl.BlockSpec((1, RES, tmm, D_FOURIER), lambda b, i: (b, 0, i, 0)),
                   nat_spec, nat_spec, four_spec, four_spec],
        out_shape=[res_shape, res_shape, res_shape, res_shape,
                   jax.ShapeDtypeStruct((B, RES, M, D_FOURIER), bf16),
                   nat_shape, nat_shape, four_shape, four_shape],
        scratch_shapes=[pltpu.VMEM((n_slabs, tm, LANES), f32)],
        compiler_params=pltpu.CompilerParams(
            dimension_semantics=("parallel", "parallel"), vmem_limit_bytes=VMEM_LIMIT),
        name="inproj",
    )(x, norm_g, w_ext, qg, kg, bd)


def _attn_kernel(q_ref, k16_ref, v16_ref, kn_ref, vn_ref, ga_ref, bias_ref, hm_ref, o_ref,
                 acc_ref, m_ref, l_ref, *, M, S):
    m0 = pl.program_id(2) * TQ
    is_a = lax.broadcasted_iota(jnp.int32, (TQ, LANES), 1) < HEAD_DIM
    mask_a = hm_ref[0:1, :]
    mask_b = hm_ref[1:2, :]

    def tile(q, k, v, bias):
        qs = jnp.concatenate([q * mask_a, q * mask_b], axis=0)
        s = lax.dot_general(qs, k, (((1,), (1,)), ((), ())), preferred_element_type=f32) + bias
        mx = jnp.max(s, axis=-1, keepdims=True)
        p = jnp.exp(s - mx)
        l = jnp.sum(p, axis=-1, keepdims=True)
        pv = jnp.dot(p.astype(bf16), v, preferred_element_type=f32)
        o = jnp.where(is_a, pv[:TQ], pv[TQ:])
        mrep = jnp.where(is_a, mx[:TQ], mx[TQ:])
        lrep = jnp.where(is_a, l[:TQ], l[TQ:])
        return o, mrep, lrep

    def merge(m_old, l_old, acc_old, o, mr, lr):
        m_new = jnp.maximum(m_old, mr)
        alpha = jnp.exp(m_old - m_new)
        beta = jnp.exp(mr - m_new)
        return m_new, l_old * alpha + lr * beta, acc_old * alpha + o * beta

    def d1_body(jj2, carry):
        row16 = pl.multiple_of(jj2 * 16, 16)
        qf = q_ref[0, 0, :, pl.ds(row16, 16), :].astype(f32)
        for par in range(2):
            jj = jj2 * 2 + par
            q = qf[:, 8 * par:8 * par + 8, :].reshape(TQ, LANES).astype(bf16)
            t0 = RES * (m0 + 8 * jj)
            ks = pl.multiple_of(jnp.clip(t0 - 64, 0, S - TK), 64)
            var = (t0 - ks) // 64
            o, mr, lr = tile(q, kn_ref[0, 0, pl.ds(ks, TK), :], vn_ref[0, 0, pl.ds(ks, TK), :],
                             bias_ref[var, 0])
            rows = pl.ds(pl.multiple_of(8 * jj, 8), 8)
            acc_ref[:, rows, :] = o.reshape(RES, 8, LANES)
            m_ref[:, rows, :] = mr.reshape(RES, 8, LANES)
            l_ref[:, rows, :] = lr.reshape(RES, 8, LANES)
        return carry

    lax.fori_loop(0, 8, d1_body, 0)

    def d4_body(it, carry):
        b = it // 4
        ml = pl.multiple_of((it % 4) * 32, 32)
        mg = m0 + ml
        ks = pl.multiple_of(jnp.clip(mg - 16, 0, M - 64), 16)
        var = (mg - ks) // 16
        q = jnp.concatenate([q_ref[0, 0, 4 * a + b, pl.ds(ml, 32), :] for a in range(4)], axis=0)
        k = jnp.concatenate([k16_ref[0, 0, 4 * a + b, pl.ds(ks, 64), :] for a in range(4)], axis=0)
        v = jnp.concatenate([v16_ref[0, 0, 4 * a + b, pl.ds(ks, 64), :] for a in range(4)], axis=0)
        o, mr, lr = tile(q, k, v, bias_ref[3 + var, 0])
        gather = lambda ref: jnp.concatenate([ref[4 * a + b, pl.ds(ml, 32), :] for a in range(4)], axis=0)
        m_new, l_new, acc_new = merge(gather(m_ref), gather(l_ref), gather(acc_ref), o, mr, lr)
        for a in range(4):
            m_ref[4 * a + b, pl.ds(ml, 32), :] = m_new[32 * a:32 * a + 32]
            l_ref[4 * a + b, pl.ds(ml, 32), :] = l_new[32 * a:32 * a + 32]
            acc_ref[4 * a + b, pl.ds(ml, 32), :] = acc_new[32 * a:32 * a + 32]
        return carry

    lax.fori_loop(0, 16, d4_body, 0)

    ks16 = pl.multiple_of(jnp.clip(m0 - 64, 0, M - TK), 64)
    var16 = (m0 - ks16) // 64

    def d16_body(r, carry):
        o, mr, lr = tile(q_ref[0, 0, r], k16_ref[0, 0, r, pl.ds(ks16, TK), :],
                         v16_ref[0, 0, r, pl.ds(ks16, TK), :], bias_ref[6 + var16, 0])
        _, l_new, acc_new = merge(m_ref[r], l_ref[r], acc_ref[r], o, mr, lr)
        o_ref[0, 0, r] = (acc_new / l_new * ga_ref[0, 0, r].astype(f32)).astype(bf16)
        return carry

    lax.fori_loop(0, RES, d16_body, 0)


def _attention(q16, k16, v16, kn, vn, ga16, bias, headmask):
    B, _, _, M, _ = q16.shape
    S = M * RES
    tile_spec = pl.BlockSpec((1, 1, RES, TQ, LANES), lambda b, j, t: (b, j, 0, t, 0))
    seq16_spec = pl.BlockSpec((1, 1, RES, M, LANES), lambda b, j, t: (b, j, 0, 0, 0))
    nat_spec = pl.BlockSpec((1, 1, S, LANES), lambda b, j, t: (b, j, 0, 0))
    return pl.pallas_call(
        functools.partial(_attn_kernel, M=M, S=S),
        grid=(B, N_PAIRS, M // TQ),
        in_specs=[tile_spec, seq16_spec, seq16_spec, nat_spec, nat_spec, tile_spec,
                  pl.BlockSpec((9, 1, 2 * TQ, TK), lambda b, j, t: (0, j, 0, 0)),
                  pl.BlockSpec((2, LANES), lambda b, j, t: (0, 0))],
        out_specs=tile_spec,
        out_shape=jax.ShapeDtypeStruct((B, N_PAIRS, RES, M, LANES), bf16),
        scratch_shapes=[pltpu.VMEM((RES, TQ, LANES), f32)] * 3,
        compiler_params=pltpu.CompilerParams(
            dimension_semantics=("parallel", "parallel", "arbitrary"), vmem_limit_bytes=VMEM_LIMIT),
        name="attn",
    )(q16, k16, v16, kn, vn, ga16, bias, headmask)


def _cmul_const(z, k, n):
    re, im = z
    if k == 0:
        return z
    if 4 * k == n:
        return im, -re
    c = math.cos(2.0 * math.pi * k / n)
    s = math.sin(2.0 * math.pi * k / n)
    return re * c + im * s, im * c - re * s


def _fft_list(xs):
    n = len(xs)
    if n == 1:
        return xs
    even = _fft_list(xs[0::2])
    odd = _fft_list(xs[1::2])
    out = [None] * n
    for k in range(n // 2):
        t = _cmul_const(odd[k], k, n)
        out[k] = (even[k][0] + t[0], even[k][1] + t[1])
        out[k + n // 2] = (even[k][0] - t[0], even[k][1] - t[1])
    return out


def _fnet_kernel(p_ref, q_ref, gf_ref, tw_ref, d_ref, o_ref, br_ref, bi_ref, *, M):
    rb = 16

    def butterfly_body(i, carry):
        row = pl.multiple_of(i * rb, rb)
        tw = tw_ref[pl.ds(row, rb), :]
        for half in range(D_FOURIER // LANES):
            lanes = slice(half * LANES, (half + 1) * LANES)
            zs = []
            for s2 in range(RES):
                rows = pl.ds(pl.multiple_of(s2 * M + row, rb), rb)
                zs.append((p_ref[0, rows, lanes].astype(f32), -q_ref[0, rows, lanes].astype(f32)))
            ys = _fft_list(zs)
            for k2 in range(RES):
                re, im = ys[k2]
                if k2:
                    c = tw[:, k2:k2 + 1]
                    s = tw[:, RES + k2:RES + k2 + 1]
                    re, im = re * c + im * s, im * c - re * s
                br_ref[k2, pl.ds(row, rb), lanes] = re.astype(bf16)
                bi_ref[k2, pl.ds(row, rb), lanes] = im.astype(bf16)
        return carry

    lax.fori_loop(0, M // rb, butterfly_body, 0)

    def dft_body(k2, carry):
        rhs = jnp.concatenate([br_ref[k2], bi_ref[k2]], axis=0)
        res = jnp.dot(d_ref[...], rhs, preferred_element_type=f32)
        o_ref[0, k2] = (res * gf_ref[0, k2].astype(f32)).astype(bf16)
        return carry

    lax.fori_loop(0, RES, dft_body, 0)


def _fourier(p, q, gf16, tw, dmat):
    B, S, _ = p.shape
    M = S // RES
    seq_spec = pl.BlockSpec((1, S, D_FOURIER), lambda b: (b, 0, 0))
    res_spec = pl.BlockSpec((1, RES, M, D_FOURIER), lambda b: (b, 0, 0, 0))
    return pl.pallas_call(
        functools.partial(_fnet_kernel, M=M),
        grid=(B,),
        in_specs=[seq_spec, seq_spec, res_spec,
                  pl.BlockSpec((M, LANES), lambda b: (0, 0)),
                  pl.BlockSpec((M, 2 * M), lambda b: (0, 0))],
        out_specs=res_spec,
        out_shape=jax.ShapeDtypeStruct((B, RES, M, D_FOURIER), bf16),
        scratch_shapes=[pltpu.VMEM((RES, M, D_FOURIER), bf16)] * 2,
        compiler_params=pltpu.CompilerParams(
            dimension_semantics=("parallel",), vmem_limit_bytes=VMEM_LIMIT),
        name="fnet",
    )(p, q, gf16, tw, dmat)


def _outproj_kernel(a_ref, f_ref, x_ref, w_ref, y_ref, scr_ref, *, tmm):
    tm = RES * tmm
    attn = jnp.concatenate([a_ref[0, j].reshape(tm, LANES) for j in range(N_PAIRS)], axis=-1)
    mix = jnp.concatenate([attn, f_ref[0].reshape(tm, D_FOURIER)], axis=-1)
    y = jnp.dot(mix, w_ref[...], preferred_element_type=f32)
    for c in range(D_MODEL // LANES):
        for r in range(RES):
            scr_ref[c, pl.ds(r, tmm, stride=RES), :] = y[r * tmm:(r + 1) * tmm, c * LANES:(c + 1) * LANES]
    for c in range(D_MODEL // LANES):
        cols = slice(c * LANES, (c + 1) * LANES)
        y_ref[0, :, cols] = x_ref[0, :, cols] + scr_ref[c]


def _outproj(attn16, four16, x, w_out, tmm=32):
    B, S, _ = x.shape
    M = S // RES
    tm = RES * tmm
    x_spec = pl.BlockSpec((1, tm, D_MODEL), lambda b, i: (b, i, 0))
    return pl.pallas_call(
        functools.partial(_outproj_kernel, tmm=tmm),
        grid=(B, M // tmm),
        in_specs=[pl.BlockSpec((1, N_PAIRS, RES, tmm, LANES), lambda b, i: (b, 0, 0, i, 0)),
                  pl.BlockSpec((1, RES, tmm, D_FOURIER), lambda b, i: (b, 0, i, 0)),
                  x_spec,
                  pl.BlockSpec((D_MODEL, D_MODEL), lambda b, i: (0, 0))],
        out_specs=x_spec,
        out_shape=jax.ShapeDtypeStruct((B, S, D_MODEL), f32),
        scratch_shapes=[pltpu.VMEM((D_MODEL // LANES, tm, LANES), f32)],
        compiler_params=pltpu.CompilerParams(
            dimension_semantics=("parallel", "parallel"), vmem_limit_bytes=VMEM_LIMIT),
        name="outproj",
    )(attn16, four16, x, w_out)


def _encoder_layer(x, norm_g, w_ext, qg, kg, bd, bias, headmask, w_out):
    tw, dmat = _dft_tables(x.shape[1])
    q16, k16, v16, ga16, gf16, kn, vn, p, q = _inproj(x, norm_g, w_ext, qg, kg, bd)
    attn16 = _attention(q16, k16, v16, kn, vn, ga16, bias, headmask)
    four16 = _fourier(p, q, gf16, jnp.asarray(tw), jnp.asarray(dmat, bf16))
    return _outproj(attn16, four16, x, w_out)


def kernel(x_prompt, x_sample, norm_g, w_in, q_norm_g, k_norm_g, rel_bias, w_four, w_out):
    depth = norm_g.shape[0]
    bias = _bias_tiles(rel_bias)
    headmask = jnp.asarray(np.stack([np.arange(LANES) < HEAD_DIM, np.arange(LANES) >= HEAD_DIM]), bf16)
    bd = jnp.asarray(np.kron(np.eye(256 // HEAD_DIM), np.full((HEAD_DIM, HEAD_DIM), 1.0 / HEAD_DIM)), bf16)
    y_prompt, y_sample = x_prompt, x_sample
    for layer in range(depth):
        w = w_in[layer]
        w_p, w_q = _fold_fourier_weights(w[:, 4 * D_ATTN:4 * D_ATTN + D_FOURIER], w_four[layer])
        w_ext = jnp.concatenate(
            [w[:, :4 * D_ATTN], w[:, 4 * D_ATTN + D_FOURIER:], w_p, w_q], axis=1).astype(bf16)
        qg = jnp.tile(q_norm_g[layer], N_HEADS)[None, :] * (1.0 / math.sqrt(HEAD_DIM))
        kg = jnp.tile(k_norm_g[layer], N_HEADS)[None, :]
        args = (norm_g[layer][None, :], w_ext, qg, kg, bd, bias, headmask, w_out[layer].astype(bf16))
        y_prompt = _encoder_layer(y_prompt, *args)
        y_sample = _encoder_layer(y_sample, *args)
    return (y_prompt, y_sample)
```

```python
import functools
import math

import numpy as np
import jax
import jax.numpy as jnp
from jax import lax
from jax.experimental import pallas as pl
from jax.experimental.pallas import tpu as pltpu

D_MODEL = 1024
HEAD_DIM = 64
N_HEADS = 12
N_PAIRS = N_HEADS // 2
D_ATTN = N_HEADS * HEAD_DIM
N_GROUPS = 4
GROUP = 64
D_FOURIER = N_GROUPS * GROUP
N_BUCKETS = 32
MAX_DISTANCE = 1024
RMS_EPS = 1e-6
NEG = -1e30
RES = 16
HALF = 64
TQ = 128
TK = 256
UNROLL = 16
LOG2E = math.log2(math.e)
LANES = 128
VMEM_LIMIT = 52 * 1024 * 1024

C_Q, C_K, C_V, C_GA, C_GF, C_P, C_QQ, C_END = 0, 768, 1536, 2304, 3072, 3328, 3584, 3840

f32 = jnp.float32
bf16 = jnp.bfloat16


def _t5_bucket_np(rel):
    half = N_BUCKETS // 2
    max_exact = half // 2
    ret = np.where(rel > 0, half, 0)
    n = np.abs(rel)
    nf = np.maximum(n, 1).astype(np.float32)
    large = max_exact + (np.log(nf / np.float32(max_exact)) / np.float32(math.log(MAX_DISTANCE / max_exact))
                         * np.float32(half - max_exact)).astype(np.int32)
    large = np.minimum(large, half - 1)
    return ret + np.where(n < max_exact, n, large)


def _bucket_tiles():
    tiles = np.zeros((3, 3, TQ, TK), np.int32)
    i = np.arange(TQ)[:, None]
    c = np.arange(TK)[None, :]
    for var in range(3):
        rel = c - 64 * var - (16 * (i % 8) + i // 8)
        tiles[0, var] = np.where(np.abs(rel) <= HALF, _t5_bucket_np(rel * 1), -1)
        rel = 4 * ((c % 64) - (i % 32) - 16 * var) + (c // 64 - i // 32)
        tiles[1, var] = np.where(np.abs(rel) <= HALF, _t5_bucket_np(rel * 4), -1)
        rel = c - 64 * var - i
        tiles[2, var] = np.where(np.abs(rel) <= HALF, _t5_bucket_np(rel * 16), -1)
    return tiles.reshape(9, TQ, TK)


def _dft_tables(seq):
    m = seq // RES
    s1 = np.arange(m)[:, None]
    k2 = np.arange(RES)[None, :]
    ang = 2.0 * np.pi * ((s1 * k2) % seq) / seq
    tw = np.zeros((m, LANES), np.float32)
    tw[:, :RES] = np.cos(ang)
    tw[:, RES:2 * RES] = np.sin(ang)
    k1 = np.arange(m)[:, None]
    ang = 2.0 * np.pi * ((k1 * np.arange(m)[None, :]) % m) / m
    scale = 1.0 / math.sqrt(seq)
    dmat = np.concatenate([np.cos(ang) * scale, np.sin(ang) * scale], axis=1)
    return tw, dmat.astype(np.float32)


def _channel_dft_blockdiag():
    c = np.arange(GROUP)
    ang = 2.0 * np.pi * ((c[:, None] * c[None, :]) % GROUP) / GROUP
    eye = np.eye(N_GROUPS)
    scale = 1.0 / math.sqrt(GROUP)
    return (np.kron(eye, np.cos(ang) * scale).astype(np.float32),
            np.kron(eye, np.sin(ang) * scale).astype(np.float32))


def _bias_kernel(rb_ref, idx_ref, out_ref):
    idx = idx_ref[0]
    pair = pl.program_id(1)
    for hh in range(2):
        head = 2 * pair + hh
        t = jnp.zeros((TQ, TK), f32)
        for b in range(N_BUCKETS):
            t = jnp.where(idx == b, rb_ref[b, head] * LOG2E, t)
        t = jnp.where(idx < 0, NEG, t)
        out_ref[0, 0, hh * TQ:(hh + 1) * TQ, :] = t


def _bias_tiles(rel_bias):
    idx = jnp.asarray(_bucket_tiles())
    return pl.pallas_call(
        _bias_kernel,
        grid=(9, N_PAIRS),
        in_specs=[pl.BlockSpec(memory_space=pltpu.SMEM),
                  pl.BlockSpec((1, TQ, TK), lambda p, j: (p, 0, 0))],
        out_specs=pl.BlockSpec((1, 1, 2 * TQ, TK), lambda p, j: (p, j, 0, 0)),
        out_shape=jax.ShapeDtypeStruct((9, N_PAIRS, 2 * TQ, TK), f32),
        name="bias_tiles",
    )(rel_bias, idx)


def _fold_kernel(wu_ref, ww_ref, cbd_ref, sbd_ref, wp_ref, wq_ref):
    hi = lax.Precision.HIGHEST
    gp = jnp.dot(cbd_ref[...], ww_ref[...], precision=hi, preferred_element_type=f32)
    gq = jnp.dot(sbd_ref[...], ww_ref[...], precision=hi, preferred_element_type=f32)
    wp_ref[...] = jnp.dot(wu_ref[...], gp, precision=hi, preferred_element_type=f32)
    wq_ref[...] = jnp.dot(wu_ref[...], gq, precision=hi, preferred_element_type=f32)


def _fold_fourier_weights(w_u, w_four):
    cbd, sbd = _channel_dft_blockdiag()
    blockmask = jnp.asarray(np.kron(np.eye(N_GROUPS), np.ones((GROUP, GROUP))).astype(np.float32))
    wwide = jnp.tile(w_four.reshape(D_FOURIER, GROUP), (1, N_GROUPS)) * blockmask
    return pl.pallas_call(
        _fold_kernel,
        out_shape=(jax.ShapeDtypeStruct((D_MODEL, D_FOURIER), f32),
                   jax.ShapeDtypeStruct((D_MODEL, D_FOURIER), f32)),
        name="fold_fourier",
    )(w_u, wwide, jnp.asarray(cbd), jnp.asarray(sbd))


def _silu(x):
    return x * (1.0 / (1.0 + jnp.exp(-x)))


def _inproj_kernel(x_ref, g_ref, w_ref, qg_ref, kg_ref, bd_ref,
                   q16_ref, k16_ref, v16_ref, ga16_ref, gf16_ref, kn_ref, vn_ref, p_ref, qq_ref,
                   scr_ref, *, tmm):
    xs = x_ref[0]
    ms = jnp.mean(xs * xs, axis=-1, keepdims=True)
    h = (xs * lax.rsqrt(ms + RMS_EPS) * g_ref[...]).astype(bf16)

    def proj(lo, hi):
        return jnp.dot(h, w_ref[:, lo:hi], preferred_element_type=f32)

    def head_norm(t, gain):
        t2 = (t * t).astype(bf16)
        msq = jnp.concatenate(
            [jnp.dot(t2[:, c * 256:(c + 1) * 256], bd_ref[...], preferred_element_type=f32)
             for c in range(D_ATTN // 256)], axis=-1)
        return t * lax.rsqrt(msq + RMS_EPS) * gain

    def residue_rows(slab, val):
        scr_ref[slab] = val
        return [scr_ref[slab, pl.ds(r, tmm, stride=RES), :] for r in range(RES)]

    def put_res(ref, val, first_slab):
        for j in range(N_PAIRS):
            rows = residue_rows(first_slab + j, val[:, j * LANES:(j + 1) * LANES])
            for r in range(RES):
                ref[0, j, r] = rows[r].astype(bf16)

    def put_nat(ref, val):
        for j in range(N_PAIRS):
            ref[0, j] = val[:, j * LANES:(j + 1) * LANES].astype(bf16)

    qn = head_norm(proj(C_Q, C_K), qg_ref[...])
    put_res(q16_ref, qn, 0)
    kn = head_norm(proj(C_K, C_V), kg_ref[...])
    put_res(k16_ref, kn, 6)
    put_nat(kn_ref, kn)
    v = proj(C_V, C_GA)
    put_res(v16_ref, v, 12)
    put_nat(vn_ref, v)
    put_res(ga16_ref, _silu(proj(C_GA, C_GF)), 18)
    gf = _silu(proj(C_GF, C_P))
    for c in range(D_FOURIER // LANES):
        rows = residue_rows(24 + c, gf[:, c * LANES:(c + 1) * LANES])
        for r in range(RES):
            gf16_ref[0, r, :, c * LANES:(c + 1) * LANES] = rows[r].astype(bf16)
    p_ref[0] = proj(C_P, C_QQ).astype(bf16)
    qq_ref[0] = proj(C_QQ, C_END).astype(bf16)


def _inproj(x, norm_g, w_ext, qg, kg, bd, tmm=32):
    B, S, _ = x.shape
    M = S // RES
    tm = RES * tmm
    res_shape = jax.ShapeDtypeStruct((B, N_PAIRS, RES, M, LANES), bf16)
    nat_shape = jax.ShapeDtypeStruct((B, N_PAIRS, S, LANES), bf16)
    four_shape = jax.ShapeDtypeStruct((B, S, D_FOURIER), bf16)
    res_spec = pl.BlockSpec((1, N_PAIRS, RES, tmm, LANES), lambda b, i: (b, 0, 0, i, 0))
    nat_spec = pl.BlockSpec((1, N_PAIRS, tm, LANES), lambda b, i: (b, 0, i, 0))
    four_spec = pl.BlockSpec((1, tm, D_FOURIER), lambda b, i: (b, i, 0))
    const = lambda shape: pl.BlockSpec(shape, lambda b, i: (0,) * len(shape))
    n_slabs = (4 * D_ATTN + D_FOURIER) // LANES
    return pl.pallas_call(
        functools.partial(_inproj_kernel, tmm=tmm),
        grid=(B, M // tmm),
        in_specs=[pl.BlockSpec((1, tm, D_MODEL), lambda b, i: (b, i, 0)),
                  const((1, D_MODEL)), const((D_MODEL, C_END)),
                  const((1, D_ATTN)), const((1, D_ATTN)), const((256, 256))],
        out_specs=[res_spec, res_spec, res_spec, res_spec,
                   pl.BlockSpec((1, RES, tmm, D_FOURIER), lambda b, i: (b, 0, i, 0)),
                   nat_spec, nat_spec, four_spec, four_spec],
        out_shape=[res_shape, res_shape, res_shape, res_shape,
                   jax.ShapeDtypeStruct((B, RES, M, D_FOURIER), bf16),
                   nat_shape, nat_shape, four_shape, four_shape],
        scratch_shapes=[pltpu.VMEM((n_slabs, tm, LANES), f32)],
        compiler_params=pltpu.CompilerParams(
            dimension_semantics=("parallel", "parallel"), vmem_limit_bytes=VMEM_LIMIT),
        name="inproj",
    )(x, norm_g, w_ext, qg, kg, bd)


def _attn_kernel(q_ref, k16_ref, v16_ref, kn_ref, vn_ref, ga_ref, bias_ref, hm_ref, o_ref,
                 acc_ref, m_ref, l_ref, *, M, S):
    m0 = pl.program_id(2) * TQ
    is_a = lax.broadcasted_iota(jnp.int32, (TQ, LANES), 1) < HEAD_DIM
    mask_a = hm_ref[0:1, :]
    mask_b = hm_ref[1:2, :]

    ones = jnp.ones((TK, LANES), bf16)

    def tile(q, k, v, bias):
        qs = jnp.concatenate([q * mask_a, q * mask_b], axis=0)
        s = lax.dot_general(qs, k, (((1,), (1,)), ((), ())), preferred_element_type=f32) + bias
        mx = jnp.max(s, axis=-1, keepdims=True)
        p = jnp.exp2(s - mx).astype(bf16)
        pv = jnp.dot(p, jnp.concatenate([v, ones], axis=1), preferred_element_type=f32)
        o = jnp.where(is_a, pv[:TQ, :LANES], pv[TQ:, :LANES])
        lrep = jnp.where(is_a, pv[:TQ, LANES:], pv[TQ:, LANES:])
        mrep = jnp.where(is_a, mx[:TQ], mx[TQ:])
        return o, mrep, lrep

    def merge(m_old, l_old, acc_old, o, mr, lr):
        m_new = jnp.maximum(m_old, mr)
        alpha = jnp.exp2(m_old - m_new)
        beta = jnp.exp2(mr - m_new)
        return m_new, l_old * alpha + lr * beta, acc_old * alpha + o * beta


    def d1_body(it, carry):
        outs = []
        for blk in range(UNROLL // 2):
            jj2 = it * (UNROLL // 2) + blk
            row16 = pl.multiple_of(jj2 * 16, 16)
            qf = q_ref[0, 0, :, pl.ds(row16, 16), :].astype(f32)
            for par in range(2):
                jj = jj2 * 2 + par
                q = qf[:, 8 * par:8 * par + 8, :].reshape(TQ, LANES).astype(bf16)
                t0 = RES * (m0 + 8 * jj)
                ks = pl.multiple_of(jnp.clip(t0 - 64, 0, S - TK), 64)
                var = (t0 - ks) // 64
                outs.append((jj, tile(q, kn_ref[0, 0, pl.ds(ks, TK), :], vn_ref[0, 0, pl.ds(ks, TK), :],
                                      bias_ref[var, 0])))
        for jj, (o, mr, lr) in outs:
            rows = pl.ds(pl.multiple_of(8 * jj, 8), 8)
            acc_ref[:, rows, :] = o.reshape(RES, 8, LANES)
            m_ref[:, rows, :] = mr.reshape(RES, 8, LANES)
            l_ref[:, rows, :] = lr.reshape(RES, 8, LANES)
        return carry

    lax.fori_loop(0, RES // UNROLL, d1_body, 0)

    def d4_body(it, carry):
        outs = []
        for u in range(UNROLL):
            t = it * UNROLL + u
            b = t // 4
            ml = pl.multiple_of((t % 4) * 32, 32)
            mg = m0 + ml
            ks = pl.multiple_of(jnp.clip(mg - 16, 0, M - 64), 16)
            var = (mg - ks) // 16
            q = jnp.concatenate([q_ref[0, 0, 4 * a + b, pl.ds(ml, 32), :] for a in range(4)], axis=0)
            k = jnp.concatenate([k16_ref[0, 0, 4 * a + b, pl.ds(ks, 64), :] for a in range(4)], axis=0)
            v = jnp.concatenate([v16_ref[0, 0, 4 * a + b, pl.ds(ks, 64), :] for a in range(4)], axis=0)
            o, mr, lr = tile(q, k, v, bias_ref[3 + var, 0])
            gather = lambda ref: jnp.concatenate([ref[4 * a + b, pl.ds(ml, 32), :] for a in range(4)], axis=0)
            outs.append((b, ml, merge(gather(m_ref), gather(l_ref), gather(acc_ref), o, mr, lr)))
        for b, ml, (m_new, l_new, acc_new) in outs:
            for a in range(4):
                m_ref[4 * a + b, pl.ds(ml, 32), :] = m_new[32 * a:32 * a + 32]
                l_ref[4 * a + b, pl.ds(ml, 32), :] = l_new[32 * a:32 * a + 32]
                acc_ref[4 * a + b, pl.ds(ml, 32), :] = acc_new[32 * a:32 * a + 32]
        return carry

    lax.fori_loop(0, RES // UNROLL, d4_body, 0)

    ks16 = pl.multiple_of(jnp.clip(m0 - 64, 0, M - TK), 64)
    var16 = (m0 - ks16) // 64

    def d16_body(it, carry):
        for u in range(UNROLL):
            r = it * UNROLL + u
            o, mr, lr = tile(q_ref[0, 0, r], k16_ref[0, 0, r, pl.ds(ks16, TK), :],
                             v16_ref[0, 0, r, pl.ds(ks16, TK), :], bias_ref[6 + var16, 0])
            _, l_new, acc_new = merge(m_ref[r], l_ref[r], acc_ref[r], o, mr, lr)
            o_ref[0, 0, r] = (acc_new / l_new * ga_ref[0, 0, r].astype(f32)).astype(bf16)
        return carry

    lax.fori_loop(0, RES // UNROLL, d16_body, 0)


def _attention(q16, k16, v16, kn, vn, ga16, bias, headmask):
    B, _, _, M, _ = q16.shape
    S = M * RES
    tile_spec = pl.BlockSpec((1, 1, RES, TQ, LANES), lambda b, j, t: (b, j, 0, t, 0))
    seq16_spec = pl.BlockSpec((1, 1, RES, M, LANES), lambda b, j, t: (b, j, 0, 0, 0))
    nat_spec = pl.BlockSpec((1, 1, S, LANES), lambda b, j, t: (b, j, 0, 0))
    return pl.pallas_call(
        functools.partial(_attn_kernel, M=M, S=S),
        grid=(B, N_PAIRS, M // TQ),
        in_specs=[tile_spec, seq16_spec, seq16_spec, nat_spec, nat_spec, tile_spec,
                  pl.BlockSpec((9, 1, 2 * TQ, TK), lambda b, j, t: (0, j, 0, 0)),
                  pl.BlockSpec((2, LANES), lambda b, j, t: (0, 0))],
        out_specs=tile_spec,
        out_shape=jax.ShapeDtypeStruct((B, N_PAIRS, RES, M, LANES), bf16),
        scratch_shapes=[pltpu.VMEM((RES, TQ, LANES), f32)] * 3,
        compiler_params=pltpu.CompilerParams(
            dimension_semantics=("parallel", "parallel", "arbitrary"), vmem_limit_bytes=VMEM_LIMIT),
        name="attn",
    )(q16, k16, v16, kn, vn, ga16, bias, headmask)


def _cmul_const(z, k, n):
    re, im = z
    if k == 0:
        return z
    if 4 * k == n:
        return im, -re
    c = math.cos(2.0 * math.pi * k / n)
    s = math.sin(2.0 * math.pi * k / n)
    return re * c + im * s, im * c - re * s


def _fft_list(xs):
    n = len(xs)
    if n == 1:
        return xs
    even = _fft_list(xs[0::2])
    odd = _fft_list(xs[1::2])
    out = [None] * n
    for k in range(n // 2):
        t = _cmul_const(odd[k], k, n)
        out[k] = (even[k][0] + t[0], even[k][1] + t[1])
        out[k + n // 2] = (even[k][0] - t[0], even[k][1] - t[1])
    return out


def _fnet_kernel(p_ref, q_ref, gf_ref, tw_ref, d_ref, o_ref, br_ref, bi_ref, *, M):
    rb = 16

    def butterfly_body(i, carry):
        row = pl.multiple_of(i * rb, rb)
        tw = tw_ref[pl.ds(row, rb), :]
        for half in range(D_FOURIER // LANES):
            lanes = slice(half * LANES, (half + 1) * LANES)
            zs = []
            for s2 in range(RES):
                rows = pl.ds(pl.multiple_of(s2 * M + row, rb), rb)
                zs.append((p_ref[0, rows, lanes].astype(f32), -q_ref[0, rows, lanes].astype(f32)))
            ys = _fft_list(zs)
            for k2 in range(RES):
                re, im = ys[k2]
                if k2:
                    c = tw[:, k2:k2 + 1]
                    s = tw[:, RES + k2:RES + k2 + 1]
                    re, im = re * c + im * s, im * c - re * s
                br_ref[k2, pl.ds(row, rb), lanes] = re.astype(bf16)
                bi_ref[k2, pl.ds(row, rb), lanes] = im.astype(bf16)
        return carry

    lax.fori_loop(0, M // rb, butterfly_body, 0)

    def dft_body(k2, carry):
        rhs = jnp.concatenate([br_ref[k2], bi_ref[k2]], axis=0)
        res = jnp.dot(d_ref[...], rhs, preferred_element_type=f32)
        o_ref[0, k2] = (res * gf_ref[0, k2].astype(f32)).astype(bf16)
        return carry

    lax.fori_loop(0, RES, dft_body, 0)


def _fourier(p, q, gf16, tw, dmat):
    B, S, _ = p.shape
    M = S // RES
    seq_spec = pl.BlockSpec((1, S, D_FOURIER), lambda b: (b, 0, 0))
    res_spec = pl.BlockSpec((1, RES, M, D_FOURIER), lambda b: (b, 0, 0, 0))
    return pl.pallas_call(
        functools.partial(_fnet_kernel, M=M),
        grid=(B,),
        in_specs=[seq_spec, seq_spec, res_spec,
                  pl.BlockSpec((M, LANES), lambda b: (0, 0)),
                  pl.BlockSpec((M, 2 * M), lambda b: (0, 0))],
        out_specs=res_spec,
        out_shape=jax.ShapeDtypeStruct((B, RES, M, D_FOURIER), bf16),
        scratch_shapes=[pltpu.VMEM((RES, M, D_FOURIER), bf16)] * 2,
        compiler_params=pltpu.CompilerParams(
            dimension_semantics=("parallel",), vmem_limit_bytes=VMEM_LIMIT),
        name="fnet",
    )(p, q, gf16, tw, dmat)


def _outproj_kernel(a_ref, f_ref, x_ref, w_ref, y_ref, scr_ref, *, tmm):
    tm = RES * tmm
    attn = jnp.concatenate([a_ref[0, j].reshape(tm, LANES) for j in range(N_PAIRS)], axis=-1)
    mix = jnp.concatenate([attn, f_ref[0].reshape(tm, D_FOURIER)], axis=-1)
    y = jnp.dot(mix, w_ref[...], preferred_element_type=f32)
    for c in range(D_MODEL // LANES):
        for r in range(RES):
            scr_ref[c, pl.ds(r, tmm, stride=RES), :] = y[r * tmm:(r + 1) * tmm, c * LANES:(c + 1) * LANES]
    for c in range(D_MODEL // LANES):
        cols = slice(c * LANES, (c + 1) * LANES)
        y_ref[0, :, cols] = x_ref[0, :, cols] + scr_ref[c]


def _outproj(attn16, four16, x, w_out, tmm=32):
    B, S, _ = x.shape
    M = S // RES
    tm = RES * tmm
    x_spec = pl.BlockSpec((1, tm, D_MODEL), lambda b, i: (b, i, 0))
    return pl.pallas_call(
        functools.partial(_outproj_kernel, tmm=tmm),
        grid=(B, M // tmm),
        in_specs=[pl.BlockSpec((1, N_PAIRS, RES, tmm, LANES), lambda b, i: (b, 0, 0, i, 0)),
                  pl.BlockSpec((1, RES, tmm, D_FOURIER), lambda b, i: (b, 0, i, 0)),
                  x_spec,
                  pl.BlockSpec((D_MODEL, D_MODEL), lambda b, i: (0, 0))],
        out_specs=x_spec,
        out_shape=jax.ShapeDtypeStruct((B, S, D_MODEL), f32),
        scratch_shapes=[pltpu.VMEM((D_MODEL // LANES, tm, LANES), f32)],
        compiler_params=pltpu.CompilerParams(
            dimension_semantics=("parallel", "parallel"), vmem_limit_bytes=VMEM_LIMIT),
        name="outproj",
    )(attn16, four16, x, w_out)


def _encoder_layer(x, norm_g, w_ext, qg, kg, bd, bias, headmask, w_out):
    tw, dmat = _dft_tables(x.shape[1])
    q16, k16, v16, ga16, gf16, kn, vn, p, q = _inproj(x, norm_g, w_ext, qg, kg, bd)
    attn16 = _attention(q16, k16, v16, kn, vn, ga16, bias, headmask)
    four16 = _fourier(p, q, gf16, jnp.asarray(tw), jnp.asarray(dmat, bf16))
    return _outproj(attn16, four16, x, w_out)


def _layer_params(norm_g, w_in, q_norm_g, k_norm_g, rel_bias, w_four, w_out):
    bias = _bias_tiles(rel_bias)
    headmask = jnp.asarray(np.stack([np.arange(LANES) < HEAD_DIM, np.arange(LANES) >= HEAD_DIM]), bf16)
    bd = jnp.asarray(np.kron(np.eye(256 // HEAD_DIM), np.full((HEAD_DIM, HEAD_DIM), 1.0 / HEAD_DIM)), bf16)
    w_p, w_q = _fold_fourier_weights(w_in[:, 4 * D_ATTN:4 * D_ATTN + D_FOURIER], w_four)
    w_ext = jnp.concatenate(
        [w_in[:, :4 * D_ATTN], w_in[:, 4 * D_ATTN + D_FOURIER:], w_p, w_q], axis=1).astype(bf16)
    qg = jnp.tile(q_norm_g, N_HEADS)[None, :] * (LOG2E / math.sqrt(HEAD_DIM))
    kg = jnp.tile(k_norm_g, N_HEADS)[None, :]
    return (norm_g[None, :], w_ext, qg, kg, bd, bias, headmask, w_out.astype(bf16))


def kernel(x_prompt, x_sample, norm_g, w_in, q_norm_g, k_norm_g, rel_bias, w_four, w_out):
    y_prompt, y_sample = x_prompt, x_sample
    for layer in range(norm_g.shape[0]):
        args = _layer_params(norm_g[layer], w_in[layer], q_norm_g[layer], k_norm_g[layer], rel_bias,
                             w_four[layer], w_out[layer])
        y_prompt = _encoder_layer(y_prompt, *args)
        y_sample = _encoder_layer(y_sample, *args)
    return (y_prompt, y_sample)
```

```python
import functools
import math

import numpy as np
import jax
import jax.numpy as jnp
from jax import lax
from jax.experimental import pallas as pl
from jax.experimental.pallas import tpu as pltpu

D_MODEL = 1024
HEAD_DIM = 64
N_HEADS = 12
N_PAIRS = N_HEADS // 2
D_ATTN = N_HEADS * HEAD_DIM
N_GROUPS = 4
GROUP = 64
D_FOURIER = N_GROUPS * GROUP
N_BUCKETS = 32
MAX_DISTANCE = 1024
RMS_EPS = 1e-6
NEG = -1e30
RES = 16
HALF = 64
TQ = 128
TK = 256
UNROLL = 16
LOG2E = math.log2(math.e)
LANES = 128
PITCH = 24
VMEM_LIMIT = 52 * 1024 * 1024

C_Q, C_K, C_V, C_GA, C_GF, C_P, C_QQ, C_END = 0, 768, 1536, 2304, 3072, 3328, 3584, 3840

f32 = jnp.float32
bf16 = jnp.bfloat16


def _t5_bucket_np(rel):
    half = N_BUCKETS // 2
    max_exact = half // 2
    ret = np.where(rel > 0, half, 0)
    n = np.abs(rel)
    nf = np.maximum(n, 1).astype(np.float32)
    large = max_exact + (np.log(nf / np.float32(max_exact)) / np.float32(math.log(MAX_DISTANCE / max_exact))
                         * np.float32(half - max_exact)).astype(np.int32)
    large = np.minimum(large, half - 1)
    return ret + np.where(n < max_exact, n, large)


def _bucket_tiles():
    tiles = np.zeros((3, 3, TQ, TK), np.int32)
    i = np.arange(TQ)[:, None]
    c = np.arange(TK)[None, :]
    for var in range(3):
        rel = c - 64 * var - (16 * (i % 8) + i // 8)
        tiles[0, var] = np.where(np.abs(rel) <= HALF, _t5_bucket_np(rel * 1), -1)
        rel = 4 * ((c % 64) - (i % 32) - 16 * var) + (c // 64 - i // 32)
        tiles[1, var] = np.where(np.abs(rel) <= HALF, _t5_bucket_np(rel * 4), -1)
        rel = c - 64 * var - i
        tiles[2, var] = np.where(np.abs(rel) <= HALF, _t5_bucket_np(rel * 16), -1)
    return tiles.reshape(9, TQ, TK)


def _dft_tables(seq):
    m = seq // RES
    s1 = np.arange(m)[:, None]
    k2 = np.arange(RES)[None, :]
    ang = 2.0 * np.pi * ((s1 * k2) % seq) / seq
    tw = np.zeros((m, LANES), np.float32)
    tw[:, :RES] = np.cos(ang)
    tw[:, RES:2 * RES] = np.sin(ang)
    k1 = np.arange(m)[:, None]
    ang = 2.0 * np.pi * ((k1 * np.arange(m)[None, :]) % m) / m
    scale = 1.0 / math.sqrt(seq)
    dmat = np.concatenate([np.cos(ang) * scale, np.sin(ang) * scale], axis=1)
    return tw, dmat.astype(np.float32)


def _channel_dft_blockdiag():
    c = np.arange(GROUP)
    ang = 2.0 * np.pi * ((c[:, None] * c[None, :]) % GROUP) / GROUP
    eye = np.eye(N_GROUPS)
    scale = 1.0 / math.sqrt(GROUP)
    return (np.kron(eye, np.cos(ang) * scale).astype(np.float32),
            np.kron(eye, np.sin(ang) * scale).astype(np.float32))


def _bias_kernel(rb_ref, idx_ref, out_ref):
    idx = idx_ref[0]
    pair = pl.program_id(1)
    for hh in range(2):
        head = 2 * pair + hh
        t = jnp.zeros((TQ, TK), f32)
        for b in range(N_BUCKETS):
            t = jnp.where(idx == b, rb_ref[b, head] * LOG2E, t)
        t = jnp.where(idx < 0, NEG, t)
        out_ref[0, 0, hh * TQ:(hh + 1) * TQ, :] = t


def _bias_tiles(rel_bias):
    idx = jnp.asarray(_bucket_tiles())
    return pl.pallas_call(
        _bias_kernel,
        grid=(9, N_PAIRS),
        in_specs=[pl.BlockSpec(memory_space=pltpu.SMEM),
                  pl.BlockSpec((1, TQ, TK), lambda p, j: (p, 0, 0))],
        out_specs=pl.BlockSpec((1, 1, 2 * TQ, TK), lambda p, j: (p, j, 0, 0)),
        out_shape=jax.ShapeDtypeStruct((9, N_PAIRS, 2 * TQ, TK), f32),
        name="bias_tiles",
    )(rel_bias, idx)


def _fold_kernel(wu_ref, ww_ref, cbd_ref, sbd_ref, wp_ref, wq_ref):
    hi = lax.Precision.HIGHEST
    gp = jnp.dot(cbd_ref[...], ww_ref[...], precision=hi, preferred_element_type=f32)
    gq = jnp.dot(sbd_ref[...], ww_ref[...], precision=hi, preferred_element_type=f32)
    wp_ref[...] = jnp.dot(wu_ref[...], gp, precision=hi, preferred_element_type=f32)
    wq_ref[...] = jnp.dot(wu_ref[...], gq, precision=hi, preferred_element_type=f32)


def _fold_fourier_weights(w_u, w_four):
    cbd, sbd = _channel_dft_blockdiag()
    blockmask = jnp.asarray(np.kron(np.eye(N_GROUPS), np.ones((GROUP, GROUP))).astype(np.float32))
    wwide = jnp.tile(w_four.reshape(D_FOURIER, GROUP), (1, N_GROUPS)) * blockmask
    return pl.pallas_call(
        _fold_kernel,
        out_shape=(jax.ShapeDtypeStruct((D_MODEL, D_FOURIER), f32),
                   jax.ShapeDtypeStruct((D_MODEL, D_FOURIER), f32)),
        name="fold_fourier",
    )(w_u, wwide, jnp.asarray(cbd), jnp.asarray(sbd))


def _silu(x):
    return x * (1.0 / (1.0 + jnp.exp(-x)))


def _inproj_kernel(x_ref, g_ref, w_ref, qg_ref, kg_ref, bd_ref,
                   q16_ref, k16_ref, v16_ref, ga16_ref, gf16_ref, kn_ref, vn_ref, p_ref, qq_ref,
                   scr_ref, *, tmm):
    xs = x_ref[0]
    ms = jnp.mean(xs * xs, axis=-1, keepdims=True)
    h = (xs * lax.rsqrt(ms + RMS_EPS) * g_ref[...]).astype(bf16)

    def proj(lo, hi):
        return jnp.dot(h, w_ref[:, lo:hi], preferred_element_type=f32)

    def head_norm(t, gain):
        t2 = (t * t).astype(bf16)
        msq = jnp.concatenate(
            [jnp.dot(t2[:, c * 256:(c + 1) * 256], bd_ref[...], preferred_element_type=f32)
             for c in range(D_ATTN // 256)], axis=-1)
        return t * lax.rsqrt(msq + RMS_EPS) * gain

    def residue_rows(slab, val):
        for g in range(tmm):
            scr_ref[slab, g * PITCH:g * PITCH + RES, :] = val[g * RES:(g + 1) * RES]
        return [scr_ref[slab, pl.ds(r, tmm, stride=PITCH), :] for r in range(RES)]

    def put_res(ref, val, first_slab):
        for j in range(N_PAIRS):
            rows = residue_rows(first_slab + j, val[:, j * LANES:(j + 1) * LANES])
            for r in range(RES):
                ref[0, j, r] = rows[r].astype(bf16)

    def put_nat(ref, val):
        for j in range(N_PAIRS):
            ref[0, j] = val[:, j * LANES:(j + 1) * LANES].astype(bf16)

    qn = head_norm(proj(C_Q, C_K), qg_ref[...])
    put_res(q16_ref, qn, 0)
    kn = head_norm(proj(C_K, C_V), kg_ref[...])
    put_res(k16_ref, kn, 6)
    put_nat(kn_ref, kn)
    v = proj(C_V, C_GA)
    put_res(v16_ref, v, 12)
    put_nat(vn_ref, v)
    put_res(ga16_ref, _silu(proj(C_GA, C_GF)), 18)
    gf = _silu(proj(C_GF, C_P))
    for c in range(D_FOURIER // LANES):
        rows = residue_rows(24 + c, gf[:, c * LANES:(c + 1) * LANES])
        for r in range(RES):
            gf16_ref[0, r, :, c * LANES:(c + 1) * LANES] = rows[r].astype(bf16)
    p_ref[0] = proj(C_P, C_QQ).astype(bf16)
    qq_ref[0] = proj(C_QQ, C_END).astype(bf16)


def _inproj(x, norm_g, w_ext, qg, kg, bd, tmm=32):
    B, S, _ = x.shape
    M = S // RES
    tm = RES * tmm
    res_shape = jax.ShapeDtypeStruct((B, N_PAIRS, RES, M, LANES), bf16)
    nat_shape = jax.ShapeDtypeStruct((B, N_PAIRS, S, LANES), bf16)
    four_shape = jax.ShapeDtypeStruct((B, S, D_FOURIER), bf16)
    res_spec = pl.BlockSpec((1, N_PAIRS, RES, tmm, LANES), lambda b, i: (b, 0, 0, i, 0))
    nat_spec = pl.BlockSpec((1, N_PAIRS, tm, LANES), lambda b, i: (b, 0, i, 0))
    four_spec = pl.BlockSpec((1, tm, D_FOURIER), lambda b, i: (b, i, 0))
    const = lambda shape: pl.BlockSpec(shape, lambda b, i: (0,) * len(shape))
    n_slabs = (4 * D_ATTN + D_FOURIER) // LANES
    return pl.pallas_call(
        functools.partial(_inproj_kernel, tmm=tmm),
        grid=(B, M // tmm),
        in_specs=[pl.BlockSpec((1, tm, D_MODEL), lambda b, i: (b, i, 0)),
                  const((1, D_MODEL)), const((D_MODEL, C_END)),
                  const((1, D_ATTN)), const((1, D_ATTN)), const((256, 256))],
        out_specs=[res_spec, res_spec, res_spec, res_spec,
                   pl.BlockSpec((1, RES, tmm, D_FOURIER), lambda b, i: (b, 0, i, 0)),
                   nat_spec, nat_spec, four_spec, four_spec],
        out_shape=[res_shape, res_shape, res_shape, res_shape,
                   jax.ShapeDtypeStruct((B, RES, M, D_FOURIER), bf16),
                   nat_shape, nat_shape, four_shape, four_shape],
        scratch_shapes=[pltpu.VMEM((n_slabs, tmm * PITCH, LANES), f32)],
        compiler_params=pltpu.CompilerParams(
            dimension_semantics=("parallel", "parallel"), vmem_limit_bytes=VMEM_LIMIT),
        name="inproj",
    )(x, norm_g, w_ext, qg, kg, bd)


def _attn_kernel(q_ref, k16_ref, v16_ref, kn_ref, vn_ref, ga_ref, bias_ref, hm_ref, o_ref,
                 acc_ref, m_ref, l_ref, *, M, S):
    m0 = pl.program_id(2) * TQ
    is_a = lax.broadcasted_iota(jnp.int32, (TQ, LANES), 1) < HEAD_DIM
    mask_a = hm_ref[0:1, :]
    mask_b = hm_ref[1:2, :]

    ones = jnp.ones((TK, LANES), bf16)

    def tile(q, k, v, bias):
        qs = jnp.concatenate([q * mask_a, q * mask_b], axis=0)
        s = lax.dot_general(qs, k, (((1,), (1,)), ((), ())), preferred_element_type=f32) + bias
        mx = jnp.max(s, axis=-1, keepdims=True)
        p = jnp.exp2(s - mx).astype(bf16)
        pv = jnp.dot(p, jnp.concatenate([v, ones], axis=1), preferred_element_type=f32)
        o = jnp.where(is_a, pv[:TQ, :LANES], pv[TQ:, :LANES])
        lrep = jnp.where(is_a, pv[:TQ, LANES:], pv[TQ:, LANES:])
        mrep = jnp.where(is_a, mx[:TQ], mx[TQ:])
        return o, mrep, lrep

    def merge(m_old, l_old, acc_old, o, mr, lr):
        m_new = jnp.maximum(m_old, mr)
        alpha = jnp.exp2(m_old - m_new)
        beta = jnp.exp2(mr - m_new)
        return m_new, l_old * alpha + lr * beta, acc_old * alpha + o * beta


    def d1_body(it, carry):
        outs = []
        for blk in range(UNROLL // 2):
            jj2 = it * (UNROLL // 2) + blk
            row16 = pl.multiple_of(jj2 * 16, 16)
            qf = q_ref[0, 0, :, pl.ds(row16, 16), :].astype(f32)
            for par in range(2):
                jj = jj2 * 2 + par
                q = qf[:, 8 * par:8 * par + 8, :].reshape(TQ, LANES).astype(bf16)
                t0 = RES * (m0 + 8 * jj)
                ks = pl.multiple_of(jnp.clip(t0 - 64, 0, S - TK), 64)
                var = (t0 - ks) // 64
                outs.append((jj, tile(q, kn_ref[0, 0, pl.ds(ks, TK), :], vn_ref[0, 0, pl.ds(ks, TK), :],
                                      bias_ref[var, 0])))
        for jj, (o, mr, lr) in outs:
            rows = pl.ds(pl.multiple_of(8 * jj, 8), 8)
            acc_ref[:, rows, :] = o.reshape(RES, 8, LANES)
            m_ref[:, rows, :] = mr.reshape(RES, 8, LANES)
            l_ref[:, rows, :] = lr.reshape(RES, 8, LANES)
        return carry

    lax.fori_loop(0, RES // UNROLL, d1_body, 0)

    def d4_body(it, carry):
        outs = []
        for u in range(UNROLL):
            t = it * UNROLL + u
            b = t // 4
            ml = pl.multiple_of((t % 4) * 32, 32)
            mg = m0 + ml
            ks = pl.multiple_of(jnp.clip(mg - 16, 0, M - 64), 16)
            var = (mg - ks) // 16
            q = jnp.concatenate([q_ref[0, 0, 4 * a + b, pl.ds(ml, 32), :] for a in range(4)], axis=0)
            k = jnp.concatenate([k16_ref[0, 0, 4 * a + b, pl.ds(ks, 64), :] for a in range(4)], axis=0)
            v = jnp.concatenate([v16_ref[0, 0, 4 * a + b, pl.ds(ks, 64), :] for a in range(4)], axis=0)
            o, mr, lr = tile(q, k, v, bias_ref[3 + var, 0])
            gather = lambda ref: jnp.concatenate([ref[4 * a + b, pl.ds(ml, 32), :] for a in range(4)], axis=0)
            outs.append((b, ml, merge(gather(m_ref), gather(l_ref), gather(acc_ref), o, mr, lr)))
        for b, ml, (m_new, l_new, acc_new) in outs:
            for a in range(4):
                m_ref[4 * a + b, pl.ds(ml, 32), :] = m_new[32 * a:32 * a + 32]
                l_ref[4 * a + b, pl.ds(ml, 32), :] = l_new[32 * a:32 * a + 32]
                acc_ref[4 * a + b, pl.ds(ml, 32), :] = acc_new[32 * a:32 * a + 32]
        return carry

    lax.fori_loop(0, RES // UNROLL, d4_body, 0)

    ks16 = pl.multiple_of(jnp.clip(m0 - 64, 0, M - TK), 64)
    var16 = (m0 - ks16) // 64

    def d16_body(it, carry):
        for u in range(UNROLL):
            r = it * UNROLL + u
            o, mr, lr = tile(q_ref[0, 0, r], k16_ref[0, 0, r, pl.ds(ks16, TK), :],
                             v16_ref[0, 0, r, pl.ds(ks16, TK), :], bias_ref[6 + var16, 0])
            _, l_new, acc_new = merge(m_ref[r], l_ref[r], acc_ref[r], o, mr, lr)
            o_ref[0, 0, r] = (acc_new / l_new * ga_ref[0, 0, r].astype(f32)).astype(bf16)
        return carry

    lax.fori_loop(0, RES // UNROLL, d16_body, 0)


def _attention(q16, k16, v16, kn, vn, ga16, bias, headmask):
    B, _, _, M, _ = q16.shape
    S = M * RES
    tile_spec = pl.BlockSpec((1, 1, RES, TQ, LANES), lambda b, j, t: (b, j, 0, t, 0))
    seq16_spec = pl.BlockSpec((1, 1, RES, M, LANES), lambda b, j, t: (b, j, 0, 0, 0))
    nat_spec = pl.BlockSpec((1, 1, S, LANES), lambda b, j, t: (b, j, 0, 0))
    return pl.pallas_call(
        functools.partial(_attn_kernel, M=M, S=S),
        grid=(B, N_PAIRS, M // TQ),
        in_specs=[tile_spec, seq16_spec, seq16_spec, nat_spec, nat_spec, tile_spec,
                  pl.BlockSpec((9, 1, 2 * TQ, TK), lambda b, j, t: (0, j, 0, 0)),
                  pl.BlockSpec((2, LANES), lambda b, j, t: (0, 0))],
        out_specs=tile_spec,
        out_shape=jax.ShapeDtypeStruct((B, N_PAIRS, RES, M, LANES), bf16),
        scratch_shapes=[pltpu.VMEM((RES, TQ, LANES), f32)] * 3,
        compiler_params=pltpu.CompilerParams(
            dimension_semantics=("parallel", "parallel", "arbitrary"), vmem_limit_bytes=VMEM_LIMIT),
        name="attn",
    )(q16, k16, v16, kn, vn, ga16, bias, headmask)


def _cmul_const(z, k, n):
    re, im = z
    if k == 0:
        return z
    if 4 * k == n:
        return im, -re
    c = math.cos(2.0 * math.pi * k / n)
    s = math.sin(2.0 * math.pi * k / n)
    return re * c + im * s, im * c - re * s


def _fft_list(xs):
    n = len(xs)
    if n == 1:
        return xs
    even = _fft_list(xs[0::2])
    odd = _fft_list(xs[1::2])
    out = [None] * n
    for k in range(n // 2):
        t = _cmul_const(odd[k], k, n)
        out[k] = (even[k][0] + t[0], even[k][1] + t[1])
        out[k + n // 2] = (even[k][0] - t[0], even[k][1] - t[1])
    return out


def _fnet_kernel(p_ref, q_ref, gf_ref, tw_ref, d_ref, o_ref, br_ref, bi_ref, *, M):
    rb = 16

    def butterfly_body(i, carry):
        row = pl.multiple_of(i * rb, rb)
        tw = tw_ref[pl.ds(row, rb), :]
        for half in range(D_FOURIER // LANES):
            lanes = slice(half * LANES, (half + 1) * LANES)
            zs = []
            for s2 in range(RES):
                rows = pl.ds(pl.multiple_of(s2 * M + row, rb), rb)
                zs.append((p_ref[0, rows, lanes].astype(f32), -q_ref[0, rows, lanes].astype(f32)))
            ys = _fft_list(zs)
            for k2 in range(RES):
                re, im = ys[k2]
                if k2:
                    c = tw[:, k2:k2 + 1]
                    s = tw[:, RES + k2:RES + k2 + 1]
                    re, im = re * c + im * s, im * c - re * s
                br_ref[k2, pl.ds(row, rb), lanes] = re.astype(bf16)
                bi_ref[k2, pl.ds(row, rb), lanes] = im.astype(bf16)
        return carry

    lax.fori_loop(0, M // rb, butterfly_body, 0)

    def dft_body(k2, carry):
        rhs = jnp.concatenate([br_ref[k2], bi_ref[k2]], axis=0)
        res = jnp.dot(d_ref[...], rhs, preferred_element_type=f32)
        o_ref[0, k2] = (res * gf_ref[0, k2].astype(f32)).astype(bf16)
        return carry

    lax.fori_loop(0, RES, dft_body, 0)


def _fourier(p, q, gf16, tw, dmat):
    B, S, _ = p.shape
    M = S // RES
    seq_spec = pl.BlockSpec((1, S, D_FOURIER), lambda b: (b, 0, 0))
    res_spec = pl.BlockSpec((1, RES, M, D_FOURIER), lambda b: (b, 0, 0, 0))
    return pl.pallas_call(
        functools.partial(_fnet_kernel, M=M),
        grid=(B,),
        in_specs=[seq_spec, seq_spec, res_spec,
                  pl.BlockSpec((M, LANES), lambda b: (0, 0)),
                  pl.BlockSpec((M, 2 * M), lambda b: (0, 0))],
        out_specs=res_spec,
        out_shape=jax.ShapeDtypeStruct((B, RES, M, D_FOURIER), bf16),
        scratch_shapes=[pltpu.VMEM((RES, M, D_FOURIER), bf16)] * 2,
        compiler_params=pltpu.CompilerParams(
            dimension_semantics=("parallel",), vmem_limit_bytes=VMEM_LIMIT),
        name="fnet",
    )(p, q, gf16, tw, dmat)


def _outproj_kernel(a_ref, f_ref, x_ref, w_ref, y_ref, scr_ref, *, tmm):
    tm = RES * tmm
    attn = jnp.concatenate([a_ref[0, j].reshape(tm, LANES) for j in range(N_PAIRS)], axis=-1)
    mix = jnp.concatenate([attn, f_ref[0].reshape(tm, D_FOURIER)], axis=-1)
    y = jnp.dot(mix, w_ref[...], preferred_element_type=f32)
    for c in range(D_MODEL // LANES):
        for r in range(RES):
            scr_ref[c, pl.ds(r, tmm, stride=PITCH), :] = y[r * tmm:(r + 1) * tmm, c * LANES:(c + 1) * LANES]
    for c in range(D_MODEL // LANES):
        cols = slice(c * LANES, (c + 1) * LANES)
        for g in range(tmm):
            rows = slice(g * RES, (g + 1) * RES)
            y_ref[0, rows, cols] = x_ref[0, rows, cols] + scr_ref[c, g * PITCH:g * PITCH + RES, :]


def _outproj(attn16, four16, x, w_out, tmm=32):
    B, S, _ = x.shape
    M = S // RES
    tm = RES * tmm
    x_spec = pl.BlockSpec((1, tm, D_MODEL), lambda b, i: (b, i, 0))
    return pl.pallas_call(
        functools.partial(_outproj_kernel, tmm=tmm),
        grid=(B, M // tmm),
        in_specs=[pl.BlockSpec((1, N_PAIRS, RES, tmm, LANES), lambda b, i: (b, 0, 0, i, 0)),
                  pl.BlockSpec((1, RES, tmm, D_FOURIER), lambda b, i: (b, 0, i, 0)),
                  x_spec,
                  pl.BlockSpec((D_MODEL, D_MODEL), lambda b, i: (0, 0))],
        out_specs=x_spec,
        out_shape=jax.ShapeDtypeStruct((B, S, D_MODEL), f32),
        scratch_shapes=[pltpu.VMEM((D_MODEL // LANES, tmm * PITCH, LANES), f32)],
        compiler_params=pltpu.CompilerParams(
            dimension_semantics=("parallel", "parallel"), vmem_limit_bytes=VMEM_LIMIT),
        name="outproj",
    )(attn16, four16, x, w_out)


def _encoder_layer(x, norm_g, w_ext, qg, kg, bd, bias, headmask, w_out):
    tw, dmat = _dft_tables(x.shape[1])
    q16, k16, v16, ga16, gf16, kn, vn, p, q = _inproj(x, norm_g, w_ext, qg, kg, bd)
    attn16 = _attention(q16, k16, v16, kn, vn, ga16, bias, headmask)
    four16 = _fourier(p, q, gf16, jnp.asarray(tw), jnp.asarray(dmat, bf16))
    return _outproj(attn16, four16, x, w_out)


def _layer_params(norm_g, w_in, q_norm_g, k_norm_g, rel_bias, w_four, w_out):
    bias = _bias_tiles(rel_bias)
    headmask = jnp.asarray(np.stack([np.arange(LANES) < HEAD_DIM, np.arange(LANES) >= HEAD_DIM]), bf16)
    bd = jnp.asarray(np.kron(np.eye(256 // HEAD_DIM), np.full((HEAD_DIM, HEAD_DIM), 1.0 / HEAD_DIM)), bf16)
    w_p, w_q = _fold_fourier_weights(w_in[:, 4 * D_ATTN:4 * D_ATTN + D_FOURIER], w_four)
    w_ext = jnp.concatenate(
        [w_in[:, :4 * D_ATTN], w_in[:, 4 * D_ATTN + D_FOURIER:], w_p, w_q], axis=1).astype(bf16)
    qg = jnp.tile(q_norm_g, N_HEADS)[None, :] * (LOG2E / math.sqrt(HEAD_DIM))
    kg = jnp.tile(k_norm_g, N_HEADS)[None, :]
    return (norm_g[None, :], w_ext, qg, kg, bd, bias, headmask, w_out.astype(bf16))


def kernel(x_prompt, x_sample, norm_g, w_in, q_norm_g, k_norm_g, rel_bias, w_four, w_out):
    y_prompt, y_sample = x_prompt, x_sample
    for layer in range(norm_g.shape[0]):
        args = _layer_params(norm_g[layer], w_in[layer], q_norm_g[layer], k_norm_g[layer], rel_bias,
                             w_four[layer], w_out[layer])
        y_prompt = _encoder_layer(y_prompt, *args)
        y_sample = _encoder_layer(y_sample, *args)
    return (y_prompt, y_sample)
```

```python
import functools
import math

import numpy as np
import jax
import jax.numpy as jnp
from jax import lax
from jax.experimental import pallas as pl
from jax.experimental.pallas import tpu as pltpu

D_MODEL = 1024
HEAD_DIM = 64
N_HEADS = 12
N_PAIRS = N_HEADS // 2
D_ATTN = N_HEADS * HEAD_DIM
N_GROUPS = 4
GROUP = 64
D_FOURIER = N_GROUPS * GROUP
N_BUCKETS = 32
MAX_DISTANCE = 1024
RMS_EPS = 1e-6
NEG = -1e30
RES = 16
HALF = 64
TQ = 128
TK = 256
UNROLL = 16
DFT_UNROLL = 4
LOG2E = math.log2(math.e)
LANES = 128
PITCH = 24
VMEM_LIMIT = 52 * 1024 * 1024

C_Q, C_K, C_V, C_GA, C_GF, C_P, C_QQ, C_END = 0, 768, 1536, 2304, 3072, 3328, 3584, 3840

f32 = jnp.float32
bf16 = jnp.bfloat16


def _t5_bucket_np(rel):
    half = N_BUCKETS // 2
    max_exact = half // 2
    ret = np.where(rel > 0, half, 0)
    n = np.abs(rel)
    nf = np.maximum(n, 1).astype(np.float32)
    large = max_exact + (np.log(nf / np.float32(max_exact)) / np.float32(math.log(MAX_DISTANCE / max_exact))
                         * np.float32(half - max_exact)).astype(np.int32)
    large = np.minimum(large, half - 1)
    return ret + np.where(n < max_exact, n, large)


def _bucket_tiles():
    tiles = np.zeros((3, 3, TQ, TK), np.int32)
    i = np.arange(TQ)[:, None]
    c = np.arange(TK)[None, :]
    for var in range(3):
        rel = c - 64 * var - (16 * (i % 8) + i // 8)
        tiles[0, var] = np.where(np.abs(rel) <= HALF, _t5_bucket_np(rel * 1), -1)
        rel = 4 * ((c % 64) - (i % 32) - 16 * var) + (c // 64 - i // 32)
        tiles[1, var] = np.where(np.abs(rel) <= HALF, _t5_bucket_np(rel * 4), -1)
        rel = c - 64 * var - i
        tiles[2, var] = np.where(np.abs(rel) <= HALF, _t5_bucket_np(rel * 16), -1)
    return tiles.reshape(9, TQ, TK)


def _dft_tables(seq):
    m = seq // RES
    s1 = np.arange(m)[:, None]
    k2 = np.arange(RES)[None, :]
    ang = 2.0 * np.pi * ((s1 * k2) % seq) / seq
    tw = np.zeros((m, LANES), np.float32)
    tw[:, :RES] = np.cos(ang)
    tw[:, RES:2 * RES] = np.sin(ang)
    k1 = np.arange(m)[:, None]
    ang = 2.0 * np.pi * ((k1 * np.arange(m)[None, :]) % m) / m
    scale = 1.0 / math.sqrt(seq)
    dmat = np.concatenate([np.cos(ang) * scale, -np.sin(ang) * scale], axis=1)
    return tw, dmat.astype(np.float32)


def _channel_dft_blockdiag():
    c = np.arange(GROUP)
    ang = 2.0 * np.pi * ((c[:, None] * c[None, :]) % GROUP) / GROUP
    eye = np.eye(N_GROUPS)
    scale = 1.0 / math.sqrt(GROUP)
    return (np.kron(eye, np.cos(ang) * scale).astype(np.float32),
            np.kron(eye, np.sin(ang) * scale).astype(np.float32))


def _bias_kernel(rb_ref, idx_ref, out_ref, *, present):
    for t in range(len(present)):
        idx = idx_ref[t]
        hits = [(b, idx == b) for b in present[t]]
        for head in range(N_HEADS):
            tile = jnp.full((TQ, TK), NEG, f32)
            for b, hit in hits:
                tile = jnp.where(hit, rb_ref[b, head] * LOG2E, tile)
            out_ref[t, head // 2, (head % 2) * TQ:(head % 2 + 1) * TQ, :] = tile


def _bias_tiles(rel_bias):
    idx = _bucket_tiles()
    present = tuple(tuple(int(b) for b in np.unique(tile) if b >= 0) for tile in idx)
    return pl.pallas_call(
        functools.partial(_bias_kernel, present=present),
        in_specs=[pl.BlockSpec(memory_space=pltpu.SMEM), pl.BlockSpec(memory_space=pltpu.VMEM)],
        out_specs=pl.BlockSpec(memory_space=pltpu.VMEM),
        out_shape=jax.ShapeDtypeStruct((9, N_PAIRS, 2 * TQ, TK), f32),
        compiler_params=pltpu.CompilerParams(vmem_limit_bytes=VMEM_LIMIT),
        name="bias_tiles",
    )(rel_bias, jnp.asarray(idx))


def _fold_kernel(wu_ref, ww_ref, cbd_ref, sbd_ref, wp_ref, wq_ref):
    hi = lax.Precision.HIGHEST
    gp = jnp.dot(cbd_ref[...], ww_ref[...], precision=hi, preferred_element_type=f32)
    gq = jnp.dot(sbd_ref[...], ww_ref[...], precision=hi, preferred_element_type=f32)
    wp_ref[...] = jnp.dot(wu_ref[...], gp, precision=hi, preferred_element_type=f32)
    wq_ref[...] = jnp.dot(wu_ref[...], gq, precision=hi, preferred_element_type=f32)


def _fold_fourier_weights(w_u, w_four):
    cbd, sbd = _channel_dft_blockdiag()
    blockmask = jnp.asarray(np.kron(np.eye(N_GROUPS), np.ones((GROUP, GROUP))).astype(np.float32))
    wwide = jnp.tile(w_four.reshape(D_FOURIER, GROUP), (1, N_GROUPS)) * blockmask
    return pl.pallas_call(
        _fold_kernel,
        out_shape=(jax.ShapeDtypeStruct((D_MODEL, D_FOURIER), f32),
                   jax.ShapeDtypeStruct((D_MODEL, D_FOURIER), f32)),
        name="fold_fourier",
    )(w_u, wwide, jnp.asarray(cbd), jnp.asarray(sbd))


def _silu(x):
    return x * (1.0 / (1.0 + jnp.exp(-x)))


def _inproj_kernel(x_ref, g_ref, w_ref, qg_ref, kg_ref, bd_ref,
                   q16_ref, k16_ref, v16_ref, ga16_ref, gf16_ref, kn_ref, vn_ref, p_ref, qq_ref,
                   scr_ref, *, tmm):
    xs = x_ref[0]
    ms = jnp.mean(xs * xs, axis=-1, keepdims=True)
    h = (xs * lax.rsqrt(ms + RMS_EPS) * g_ref[...]).astype(bf16)

    def proj(lo, hi):
        return jnp.dot(h, w_ref[:, lo:hi], preferred_element_type=f32)

    def head_norm(t, gain):
        t2 = (t * t).astype(bf16)
        msq = jnp.concatenate(
            [jnp.dot(t2[:, c * 256:(c + 1) * 256], bd_ref[...], preferred_element_type=f32)
             for c in range(D_ATTN // 256)], axis=-1)
        return t * lax.rsqrt(msq + RMS_EPS) * gain

    def residue_rows(slab, val):
        for g in range(tmm):
            scr_ref[slab, g * PITCH:g * PITCH + RES, :] = val[g * RES:(g + 1) * RES]
        return [scr_ref[slab, pl.ds(r, tmm, stride=PITCH), :] for r in range(RES)]

    def put_res(ref, val, first_slab):
        for j in range(N_PAIRS):
            rows = residue_rows(first_slab + j, val[:, j * LANES:(j + 1) * LANES])
            for r in range(RES):
                ref[0, j, r] = rows[r].astype(bf16)

    def put_nat(ref, val):
        for j in range(N_PAIRS):
            ref[0, j] = val[:, j * LANES:(j + 1) * LANES].astype(bf16)

    qn = head_norm(proj(C_Q, C_K), qg_ref[...])
    put_res(q16_ref, qn, 0)
    kn = head_norm(proj(C_K, C_V), kg_ref[...])
    put_res(k16_ref, kn, 6)
    put_nat(kn_ref, kn)
    v = proj(C_V, C_GA)
    put_res(v16_ref, v, 12)
    put_nat(vn_ref, v)
    put_res(ga16_ref, _silu(proj(C_GA, C_GF)), 18)
    four = proj(C_GF, C_END)
    gf = _silu(four[:, :D_FOURIER])
    for c in range(D_FOURIER // LANES):
        rows = residue_rows(24 + c, gf[:, c * LANES:(c + 1) * LANES])
        for r in range(RES):
            gf16_ref[0, r, :, c * LANES:(c + 1) * LANES] = rows[r].astype(bf16)
    p_ref[0] = four[:, C_P - C_GF:C_QQ - C_GF].astype(bf16)
    qq_ref[0] = four[:, C_QQ - C_GF:].astype(bf16)


def _inproj(x, norm_g, w_ext, qg, kg, bd, tmm=32):
    B, S, _ = x.shape
    M = S // RES
    tm = RES * tmm
    res_shape = jax.ShapeDtypeStruct((B, N_PAIRS, RES, M, LANES), bf16)
    nat_shape = jax.ShapeDtypeStruct((B, N_PAIRS, S, LANES), bf16)
    four_shape = jax.ShapeDtypeStruct((B, S, D_FOURIER), bf16)
    res_spec = pl.BlockSpec((1, N_PAIRS, RES, tmm, LANES), lambda b, i: (b, 0, 0, i, 0))
    nat_spec = pl.BlockSpec((1, N_PAIRS, tm, LANES), lambda b, i: (b, 0, i, 0))
    four_spec = pl.BlockSpec((1, tm, D_FOURIER), lambda b, i: (b, i, 0))
    const = lambda shape: pl.BlockSpec(shape, lambda b, i: (0,) * len(shape))
    n_slabs = (4 * D_ATTN + D_FOURIER) // LANES
    return pl.pallas_call(
        functools.partial(_inproj_kernel, tmm=tmm),
        grid=(B, M // tmm),
        in_specs=[pl.BlockSpec((1, tm, D_MODEL), lambda b, i: (b, i, 0)),
                  const((1, D_MODEL)), const((D_MODEL, C_END)),
                  const((1, D_ATTN)), const((1, D_ATTN)), const((256, 256))],
        out_specs=[res_spec, res_spec, res_spec, res_spec,
                   pl.BlockSpec((1, RES, tmm, D_FOURIER), lambda b, i: (b, 0, i, 0)),
                   nat_spec, nat_spec, four_spec, four_spec],
        out_shape=[res_shape, res_shape, res_shape, res_shape,
                   jax.ShapeDtypeStruct((B, RES, M, D_FOURIER), bf16),
                   nat_shape, nat_shape, four_shape, four_shape],
        scratch_shapes=[pltpu.VMEM((n_slabs, tmm * PITCH, LANES), f32)],
        compiler_params=pltpu.CompilerParams(
            dimension_semantics=("parallel", "parallel"), vmem_limit_bytes=VMEM_LIMIT),
        name="inproj",
    )(x, norm_g, w_ext, qg, kg, bd)


def _attn_kernel(q_ref, k16_ref, v16_ref, kn_ref, vn_ref, ga_ref, bias_ref, hm_ref, o_ref,
                 acc_ref, m_ref, l_ref, *, M, S):
    m0 = pl.program_id(2) * TQ
    is_a = lax.broadcasted_iota(jnp.int32, (TQ, LANES), 1) < HEAD_DIM
    mask_a = hm_ref[0:1, :]
    mask_b = hm_ref[1:2, :]

    ones = jnp.ones((TK, LANES), bf16)

    def tile(q, k, v, bias):
        qs = jnp.concatenate([q * mask_a, q * mask_b], axis=0)
        s = lax.dot_general(qs, k, (((1,), (1,)), ((), ())), preferred_element_type=f32) + bias
        mx = jnp.max(s, axis=-1, keepdims=True)
        p = jnp.exp2(s - mx).astype(bf16)
        pv = jnp.dot(p, jnp.concatenate([v, ones], axis=1), preferred_element_type=f32)
        o = jnp.where(is_a, pv[:TQ, :LANES], pv[TQ:, :LANES])
        lrep = jnp.where(is_a, pv[:TQ, LANES:], pv[TQ:, LANES:])
        mrep = jnp.where(is_a, mx[:TQ], mx[TQ:])
        return o, mrep, lrep

    def merge(m_old, l_old, acc_old, o, mr, lr):
        m_new = jnp.maximum(m_old, mr)
        alpha = jnp.exp2(m_old - m_new)
        beta = jnp.exp2(mr - m_new)
        return m_new, l_old * alpha + lr * beta, acc_old * alpha + o * beta


    def d1_body(it, carry):
        outs = []
        for blk in range(UNROLL // 2):
            jj2 = it * (UNROLL // 2) + blk
            row16 = pl.multiple_of(jj2 * 16, 16)
            qf = q_ref[0, 0, :, pl.ds(row16, 16), :].astype(f32)
            for par in range(2):
                jj = jj2 * 2 + par
                q = qf[:, 8 * par:8 * par + 8, :].reshape(TQ, LANES).astype(bf16)
                t0 = RES * (m0 + 8 * jj)
                ks = pl.multiple_of(jnp.clip(t0 - 64, 0, S - TK), 64)
                var = (t0 - ks) // 64
                outs.append((jj, tile(q, kn_ref[0, 0, pl.ds(ks, TK), :], vn_ref[0, 0, pl.ds(ks, TK), :],
                                      bias_ref[var, 0])))
        for jj, (o, mr, lr) in outs:
            rows = pl.ds(pl.multiple_of(8 * jj, 8), 8)
            acc_ref[:, rows, :] = o.reshape(RES, 8, LANES)
            m_ref[:, rows, :] = mr.reshape(RES, 8, LANES)
            l_ref[:, rows, :] = lr.reshape(RES, 8, LANES)
        return carry

    lax.fori_loop(0, RES // UNROLL, d1_body, 0)

    def d4_body(it, carry):
        outs = []
        for u in range(UNROLL):
            t = it * UNROLL + u
            b = t // 4
            ml = pl.multiple_of((t % 4) * 32, 32)
            mg = m0 + ml
            ks = pl.multiple_of(jnp.clip(mg - 16, 0, M - 64), 16)
            var = (mg - ks) // 16
            q = jnp.concatenate([q_ref[0, 0, 4 * a + b, pl.ds(ml, 32), :] for a in range(4)], axis=0)
            k = jnp.concatenate([k16_ref[0, 0, 4 * a + b, pl.ds(ks, 64), :] for a in range(4)], axis=0)
            v = jnp.concatenate([v16_ref[0, 0, 4 * a + b, pl.ds(ks, 64), :] for a in range(4)], axis=0)
            o, mr, lr = tile(q, k, v, bias_ref[3 + var, 0])
            gather = lambda ref: jnp.concatenate([ref[4 * a + b, pl.ds(ml, 32), :] for a in range(4)], axis=0)
            outs.append((b, ml, merge(gather(m_ref), gather(l_ref), gather(acc_ref), o, mr, lr)))
        for b, ml, (m_new, l_new, acc_new) in outs:
            for a in range(4):
                m_ref[4 * a + b, pl.ds(ml, 32), :] = m_new[32 * a:32 * a + 32]
                l_ref[4 * a + b, pl.ds(ml, 32), :] = l_new[32 * a:32 * a + 32]
                acc_ref[4 * a + b, pl.ds(ml, 32), :] = acc_new[32 * a:32 * a + 32]
        return carry

    lax.fori_loop(0, RES // UNROLL, d4_body, 0)

    ks16 = pl.multiple_of(jnp.clip(m0 - 64, 0, M - TK), 64)
    var16 = (m0 - ks16) // 64

    def d16_body(it, carry):
        for u in range(UNROLL):
            r = it * UNROLL + u
            o, mr, lr = tile(q_ref[0, 0, r], k16_ref[0, 0, r, pl.ds(ks16, TK), :],
                             v16_ref[0, 0, r, pl.ds(ks16, TK), :], bias_ref[6 + var16, 0])
            _, l_new, acc_new = merge(m_ref[r], l_ref[r], acc_ref[r], o, mr, lr)
            o_ref[0, 0, r] = (acc_new / l_new * ga_ref[0, 0, r].astype(f32)).astype(bf16)
        return carry

    lax.fori_loop(0, RES // UNROLL, d16_body, 0)


def _attention(q16, k16, v16, kn, vn, ga16, bias, headmask):
    B, _, _, M, _ = q16.shape
    S = M * RES
    tile_spec = pl.BlockSpec((1, 1, RES, TQ, LANES), lambda b, j, t: (b, j, 0, t, 0))
    seq16_spec = pl.BlockSpec((1, 1, RES, M, LANES), lambda b, j, t: (b, j, 0, 0, 0))
    nat_spec = pl.BlockSpec((1, 1, S, LANES), lambda b, j, t: (b, j, 0, 0))
    return pl.pallas_call(
        functools.partial(_attn_kernel, M=M, S=S),
        grid=(B, N_PAIRS, M // TQ),
        in_specs=[tile_spec, seq16_spec, seq16_spec, nat_spec, nat_spec, tile_spec,
                  pl.BlockSpec((9, 1, 2 * TQ, TK), lambda b, j, t: (0, j, 0, 0)),
                  pl.BlockSpec((2, LANES), lambda b, j, t: (0, 0))],
        out_specs=tile_spec,
        out_shape=jax.ShapeDtypeStruct((B, N_PAIRS, RES, M, LANES), bf16),
        scratch_shapes=[pltpu.VMEM((RES, TQ, LANES), f32)] * 3,
        compiler_params=pltpu.CompilerParams(
            dimension_semantics=("parallel", "parallel", "arbitrary"), vmem_limit_bytes=VMEM_LIMIT),
        name="attn",
    )(q16, k16, v16, kn, vn, ga16, bias, headmask)


def _cmul_const(z, k, n):
    re, im = z
    if k == 0:
        return z
    if 4 * k == n:
        return -im, re
    c = math.cos(2.0 * math.pi * k / n)
    s = math.sin(2.0 * math.pi * k / n)
    return re * c - im * s, im * c + re * s


def _fft_list(xs):
    n = len(xs)
    if n == 1:
        return xs
    even = _fft_list(xs[0::2])
    odd = _fft_list(xs[1::2])
    out = [None] * n
    for k in range(n // 2):
        t = _cmul_const(odd[k], k, n)
        out[k] = (even[k][0] + t[0], even[k][1] + t[1])
        out[k + n // 2] = (even[k][0] - t[0], even[k][1] - t[1])
    return out


def _fnet_kernel(p_ref, q_ref, gf_ref, tw_ref, d_ref, o_ref, br_ref, bi_ref, *, M):
    rb = 16

    def butterfly_body(i, carry):
        row = pl.multiple_of(i * rb, rb)
        tw = tw_ref[pl.ds(row, rb), :]
        for half in range(D_FOURIER // LANES):
            lanes = slice(half * LANES, (half + 1) * LANES)
            zs = []
            for s2 in range(RES):
                rows = pl.ds(pl.multiple_of(s2 * M + row, rb), rb)
                zs.append((p_ref[0, rows, lanes].astype(f32), q_ref[0, rows, lanes].astype(f32)))
            ys = _fft_list(zs)
            for k2 in range(RES):
                re, im = ys[k2]
                if k2:
                    c = tw[:, k2:k2 + 1]
                    s = tw[:, RES + k2:RES + k2 + 1]
                    re, im = re * c - im * s, im * c + re * s
                br_ref[k2, pl.ds(row, rb), lanes] = re.astype(bf16)
                bi_ref[k2, pl.ds(row, rb), lanes] = im.astype(bf16)
        return carry

    lax.fori_loop(0, M // rb, butterfly_body, 0)

    def dft_body(i, carry):
        for u in range(DFT_UNROLL):
            k2 = i * DFT_UNROLL + u
            rhs = jnp.concatenate([br_ref[k2], bi_ref[k2]], axis=0)
            res = jnp.dot(d_ref[...], rhs, preferred_element_type=f32)
            o_ref[0, k2] = (res * gf_ref[0, k2].astype(f32)).astype(bf16)
        return carry

    lax.fori_loop(0, RES // DFT_UNROLL, dft_body, 0)


def _fourier(p, q, gf16, tw, dmat):
    B, S, _ = p.shape
    M = S // RES
    seq_spec = pl.BlockSpec((1, S, D_FOURIER), lambda b: (b, 0, 0))
    res_spec = pl.BlockSpec((1, RES, M, D_FOURIER), lambda b: (b, 0, 0, 0))
    return pl.pallas_call(
        functools.partial(_fnet_kernel, M=M),
        grid=(B,),
        in_specs=[seq_spec, seq_spec, res_spec,
                  pl.BlockSpec((M, LANES), lambda b: (0, 0)),
                  pl.BlockSpec((M, 2 * M), lambda b: (0, 0))],
        out_specs=res_spec,
        out_shape=jax.ShapeDtypeStruct((B, RES, M, D_FOURIER), bf16),
        scratch_shapes=[pltpu.VMEM((RES, M, D_FOURIER), bf16)] * 2,
        compiler_params=pltpu.CompilerParams(
            dimension_semantics=("parallel",), vmem_limit_bytes=VMEM_LIMIT),
        name="fnet",
    )(p, q, gf16, tw, dmat)


def _outproj_kernel(a_ref, f_ref, x_ref, w_ref, y_ref, scr_ref, *, tmm):
    tm = RES * tmm
    attn = jnp.concatenate([a_ref[0, j].reshape(tm, LANES) for j in range(N_PAIRS)], axis=-1)
    mix = jnp.concatenate([attn, f_ref[0].reshape(tm, D_FOURIER)], axis=-1)
    y = jnp.dot(mix, w_ref[...], preferred_element_type=f32)
    for c in range(D_MODEL // LANES):
        for r in range(RES):
            scr_ref[c, pl.ds(r, tmm, stride=PITCH), :] = y[r * tmm:(r + 1) * tmm, c * LANES:(c + 1) * LANES]
    for c in range(D_MODEL // LANES):
        cols = slice(c * LANES, (c + 1) * LANES)
        for g in range(tmm):
            rows = slice(g * RES, (g + 1) * RES)
            y_ref[0, rows, cols] = x_ref[0, rows, cols] + scr_ref[c, g * PITCH:g * PITCH + RES, :]


def _outproj(attn16, four16, x, w_out, tmm=64):
    B, S, _ = x.shape
    M = S // RES
    tm = RES * tmm
    x_spec = pl.BlockSpec((1, tm, D_MODEL), lambda b, i: (b, i, 0))
    return pl.pallas_call(
        functools.partial(_outproj_kernel, tmm=tmm),
        grid=(B, M // tmm),
        in_specs=[pl.BlockSpec((1, N_PAIRS, RES, tmm, LANES), lambda b, i: (b, 0, 0, i, 0)),
                  pl.BlockSpec((1, RES, tmm, D_FOURIER), lambda b, i: (b, 0, i, 0)),
                  x_spec,
                  pl.BlockSpec((D_MODEL, D_MODEL), lambda b, i: (0, 0))],
        out_specs=x_spec,
        out_shape=jax.ShapeDtypeStruct((B, S, D_MODEL), f32),
        scratch_shapes=[pltpu.VMEM((D_MODEL // LANES, tmm * PITCH, LANES), f32)],
        compiler_params=pltpu.CompilerParams(
            dimension_semantics=("parallel", "parallel"), vmem_limit_bytes=VMEM_LIMIT),
        name="outproj",
    )(attn16, four16, x, w_out)


def _encoder_layer(x, norm_g, w_ext, qg, kg, bd, bias, headmask, w_out):
    tw, dmat = _dft_tables(x.shape[1])
    q16, k16, v16, ga16, gf16, kn, vn, p, q = _inproj(x, norm_g, w_ext, qg, kg, bd)
    attn16 = _attention(q16, k16, v16, kn, vn, ga16, bias, headmask)
    four16 = _fourier(p, q, gf16, jnp.asarray(tw), jnp.asarray(dmat, bf16))
    return _outproj(attn16, four16, x, w_out)


def _layer_params(norm_g, w_in, q_norm_g, k_norm_g, rel_bias, w_four, w_out):
    bias = _bias_tiles(rel_bias)
    headmask = jnp.asarray(np.stack([np.arange(LANES) < HEAD_DIM, np.arange(LANES) >= HEAD_DIM]), bf16)
    bd = jnp.asarray(np.kron(np.eye(256 // HEAD_DIM), np.full((HEAD_DIM, HEAD_DIM), 1.0 / HEAD_DIM)), bf16)
    w_p, w_q = _fold_fourier_weights(w_in[:, 4 * D_ATTN:4 * D_ATTN + D_FOURIER], w_four)
    w_ext = jnp.concatenate(
        [w_in[:, :4 * D_ATTN], w_in[:, 4 * D_ATTN + D_FOURIER:], w_p, w_q], axis=1).astype(bf16)
    qg = jnp.tile(q_norm_g, N_HEADS)[None, :] * (LOG2E / math.sqrt(HEAD_DIM))
    kg = jnp.tile(k_norm_g, N_HEADS)[None, :]
    return (norm_g[None, :], w_ext, qg, kg, bd, bias, headmask, w_out.astype(bf16))


def kernel(x_prompt, x_sample, norm_g, w_in, q_norm_g, k_norm_g, rel_bias, w_four, w_out):
    y_prompt, y_sample = x_prompt, x_sample
    for layer in range(norm_g.shape[0]):
        args = _layer_params(norm_g[layer], w_in[layer], q_norm_g[layer], k_norm_g[layer], rel_bias,
                             w_four[layer], w_out[layer])
        y_prompt = _encoder_layer(y_prompt, *args)
        y_sample = _encoder_layer(y_sample, *args)
    return (y_prompt, y_sample)
```

```python
import functools
import math

import numpy as np
import jax
import jax.numpy as jnp
from jax import lax
from jax.experimental import pallas as pl
from jax.experimental.pallas import tpu as pltpu

D_MODEL = 1024
HEAD_DIM = 64
N_HEADS = 12
N_PAIRS = N_HEADS // 2
D_ATTN = N_HEADS * HEAD_DIM
N_GROUPS = 4
GROUP = 64
D_FOURIER = N_GROUPS * GROUP
N_BUCKETS = 32
MAX_DISTANCE = 1024
RMS_EPS = 1e-6
NEG = -1e30
RES = 16
HALF = 64
TQ = 128
TK = 256
SUBTILES = 2
DFT_UNROLL = 4
LOG2E = math.log2(math.e)
LANES = 128
PITCH = 24
VMEM_LIMIT = 52 * 1024 * 1024

C_Q, C_K, C_V, C_GA, C_GF, C_P, C_QQ, C_END = 0, 768, 1536, 2304, 3072, 3328, 3584, 3840

f32 = jnp.float32
bf16 = jnp.bfloat16


def _t5_bucket_np(rel):
    half = N_BUCKETS // 2
    max_exact = half // 2
    ret = np.where(rel > 0, half, 0)
    n = np.abs(rel)
    nf = np.maximum(n, 1).astype(np.float32)
    large = max_exact + (np.log(nf / np.float32(max_exact)) / np.float32(math.log(MAX_DISTANCE / max_exact))
                         * np.float32(half - max_exact)).astype(np.int32)
    large = np.minimum(large, half - 1)
    return ret + np.where(n < max_exact, n, large)


def _bucket_tiles():
    tiles = np.zeros((3, 3, TQ, TK), np.int32)
    i = np.arange(TQ)[:, None]
    c = np.arange(TK)[None, :]
    for var in range(3):
        rel = c - 64 * var - (16 * (i % 8) + i // 8)
        tiles[0, var] = np.where(np.abs(rel) <= HALF, _t5_bucket_np(rel * 1), -1)
        rel = 4 * ((c % 64) - (i % 32) - 16 * var) + (c // 64 - i // 32)
        tiles[1, var] = np.where(np.abs(rel) <= HALF, _t5_bucket_np(rel * 4), -1)
        rel = c - 64 * var - i
        tiles[2, var] = np.where(np.abs(rel) <= HALF, _t5_bucket_np(rel * 16), -1)
    return tiles.reshape(9, TQ, TK)


def _dft_tables(seq):
    m = seq // RES
    s1 = np.arange(m)[:, None]
    k2 = np.arange(RES)[None, :]
    ang = 2.0 * np.pi * ((s1 * k2) % seq) / seq
    tw = np.zeros((m, LANES), np.float32)
    tw[:, :RES] = np.cos(ang)
    tw[:, RES:2 * RES] = np.sin(ang)
    k1 = np.arange(m)[:, None]
    ang = 2.0 * np.pi * ((k1 * np.arange(m)[None, :]) % m) / m
    scale = 1.0 / math.sqrt(seq)
    dmat = np.concatenate([np.cos(ang) * scale, -np.sin(ang) * scale], axis=1)
    return tw, dmat.astype(np.float32)


def _channel_dft_blockdiag():
    c = np.arange(GROUP)
    ang = 2.0 * np.pi * ((c[:, None] * c[None, :]) % GROUP) / GROUP
    eye = np.eye(N_GROUPS)
    scale = 1.0 / math.sqrt(GROUP)
    return (np.kron(eye, np.cos(ang) * scale).astype(np.float32),
            np.kron(eye, np.sin(ang) * scale).astype(np.float32))


def _bias_kernel(rb_ref, idx_ref, out_ref, *, present):
    for t in range(len(present)):
        idx = idx_ref[t]
        hits = [(b, idx == b) for b in present[t]]
        for head in range(N_HEADS):
            tile = jnp.full((TQ, TK), NEG, f32)
            for b, hit in hits:
                tile = jnp.where(hit, rb_ref[b, head] * LOG2E, tile)
            out_ref[t, head // 2, (head % 2) * TQ:(head % 2 + 1) * TQ, :] = tile


def _bias_tiles(rel_bias):
    idx = _bucket_tiles()
    present = tuple(tuple(int(b) for b in np.unique(tile) if b >= 0) for tile in idx)
    return pl.pallas_call(
        functools.partial(_bias_kernel, present=present),
        in_specs=[pl.BlockSpec(memory_space=pltpu.SMEM), pl.BlockSpec(memory_space=pltpu.VMEM)],
        out_specs=pl.BlockSpec(memory_space=pltpu.VMEM),
        out_shape=jax.ShapeDtypeStruct((9, N_PAIRS, 2 * TQ, TK), f32),
        compiler_params=pltpu.CompilerParams(vmem_limit_bytes=VMEM_LIMIT),
        name="bias_tiles",
    )(rel_bias, jnp.asarray(idx))


def _fold_kernel(wu_ref, ww_ref, cbd_ref, sbd_ref, wp_ref, wq_ref):
    hi = lax.Precision.HIGHEST
    gp = jnp.dot(cbd_ref[...], ww_ref[...], precision=hi, preferred_element_type=f32)
    gq = jnp.dot(sbd_ref[...], ww_ref[...], precision=hi, preferred_element_type=f32)
    wp_ref[...] = jnp.dot(wu_ref[...], gp, precision=hi, preferred_element_type=f32)
    wq_ref[...] = jnp.dot(wu_ref[...], gq, precision=hi, preferred_element_type=f32)


def _fold_fourier_weights(w_u, w_four):
    cbd, sbd = _channel_dft_blockdiag()
    blockmask = jnp.asarray(np.kron(np.eye(N_GROUPS), np.ones((GROUP, GROUP))).astype(np.float32))
    wwide = jnp.tile(w_four.reshape(D_FOURIER, GROUP), (1, N_GROUPS)) * blockmask
    return pl.pallas_call(
        _fold_kernel,
        out_shape=(jax.ShapeDtypeStruct((D_MODEL, D_FOURIER), f32),
                   jax.ShapeDtypeStruct((D_MODEL, D_FOURIER), f32)),
        name="fold_fourier",
    )(w_u, wwide, jnp.asarray(cbd), jnp.asarray(sbd))


def _silu(x):
    return x * (1.0 / (1.0 + jnp.exp(-x)))


def _inproj_kernel(x_ref, g_ref, w_ref, qg_ref, kg_ref, bd_ref,
                   q16_ref, k16_ref, v16_ref, ga16_ref, gf16_ref, kn_ref, vn_ref, p_ref, qq_ref,
                   scr_ref, *, tmm):
    xs = x_ref[0]
    ms = jnp.mean(xs * xs, axis=-1, keepdims=True)
    h = (xs * lax.rsqrt(ms + RMS_EPS) * g_ref[...]).astype(bf16)

    def proj(lo, hi):
        return jnp.dot(h, w_ref[:, lo:hi], preferred_element_type=f32)

    def head_norm(t, gain):
        t2 = (t * t).astype(bf16)
        msq = jnp.concatenate(
            [jnp.dot(t2[:, c * 256:(c + 1) * 256], bd_ref[...], preferred_element_type=f32)
             for c in range(D_ATTN // 256)], axis=-1)
        return t * lax.rsqrt(msq + RMS_EPS) * gain

    def residue_rows(slab, val):
        for g in range(tmm):
            scr_ref[slab, g * PITCH:g * PITCH + RES, :] = val[g * RES:(g + 1) * RES]
        return [scr_ref[slab, pl.ds(r, tmm, stride=PITCH), :] for r in range(RES)]

    def put_res(ref, val, first_slab):
        for j in range(N_PAIRS):
            rows = residue_rows(first_slab + j, val[:, j * LANES:(j + 1) * LANES])
            for r in range(RES):
                ref[0, j, r] = rows[r].astype(bf16)

    def put_nat(ref, val):
        for j in range(N_PAIRS):
            ref[0, j] = val[:, j * LANES:(j + 1) * LANES].astype(bf16)

    qn = head_norm(proj(C_Q, C_K), qg_ref[...])
    put_res(q16_ref, qn, 0)
    kn = head_norm(proj(C_K, C_V), kg_ref[...])
    put_res(k16_ref, kn, 6)
    put_nat(kn_ref, kn)
    v = proj(C_V, C_GA)
    put_res(v16_ref, v, 12)
    put_nat(vn_ref, v)
    put_res(ga16_ref, _silu(proj(C_GA, C_GF)), 18)
    four = proj(C_GF, C_END)
    gf = _silu(four[:, :D_FOURIER])
    for c in range(D_FOURIER // LANES):
        rows = residue_rows(24 + c, gf[:, c * LANES:(c + 1) * LANES])
        for r in range(RES):
            gf16_ref[0, r, :, c * LANES:(c + 1) * LANES] = rows[r].astype(bf16)
    p_ref[0] = four[:, C_P - C_GF:C_QQ - C_GF].astype(bf16)
    qq_ref[0] = four[:, C_QQ - C_GF:].astype(bf16)


def _inproj(x, norm_g, w_ext, qg, kg, bd, tmm=32):
    B, S, _ = x.shape
    M = S // RES
    tm = RES * tmm
    res_shape = jax.ShapeDtypeStruct((B, N_PAIRS, RES, M, LANES), bf16)
    nat_shape = jax.ShapeDtypeStruct((B, N_PAIRS, S, LANES), bf16)
    four_shape = jax.ShapeDtypeStruct((B, S, D_FOURIER), bf16)
    res_spec = pl.BlockSpec((1, N_PAIRS, RES, tmm, LANES), lambda b, i: (b, 0, 0, i, 0))
    nat_spec = pl.BlockSpec((1, N_PAIRS, tm, LANES), lambda b, i: (b, 0, i, 0))
    four_spec = pl.BlockSpec((1, tm, D_FOURIER), lambda b, i: (b, i, 0))
    const = lambda shape: pl.BlockSpec(shape, lambda b, i: (0,) * len(shape))
    n_slabs = (4 * D_ATTN + D_FOURIER) // LANES
    return pl.pallas_call(
        functools.partial(_inproj_kernel, tmm=tmm),
        grid=(B, M // tmm),
        in_specs=[pl.BlockSpec((1, tm, D_MODEL), lambda b, i: (b, i, 0)),
                  const((1, D_MODEL)), const((D_MODEL, C_END)),
                  const((1, D_ATTN)), const((1, D_ATTN)), const((256, 256))],
        out_specs=[res_spec, res_spec, res_spec, res_spec,
                   pl.BlockSpec((1, RES, tmm, D_FOURIER), lambda b, i: (b, 0, i, 0)),
                   nat_spec, nat_spec, four_spec, four_spec],
        out_shape=[res_shape, res_shape, res_shape, res_shape,
                   jax.ShapeDtypeStruct((B, RES, M, D_FOURIER), bf16),
                   nat_shape, nat_shape, four_shape, four_shape],
        scratch_shapes=[pltpu.VMEM((n_slabs, tmm * PITCH, LANES), f32)],
        compiler_params=pltpu.CompilerParams(
            dimension_semantics=("parallel", "parallel"), vmem_limit_bytes=VMEM_LIMIT),
        name="inproj",
    )(x, norm_g, w_ext, qg, kg, bd)


def _attn_kernel(q_ref, k16_ref, v16_ref, kn_ref, vn_ref, ga_ref, bias_ref, hm_ref, o_ref, *, M, S):
    is_a = lax.broadcasted_iota(jnp.int32, (TQ, LANES), 1) < HEAD_DIM
    mask_a = hm_ref[0:1, :]
    mask_b = hm_ref[1:2, :]
    ones = jnp.ones((TK, LANES), bf16)

    def tile(q, k, v, bias):
        qs = jnp.concatenate([q * mask_a, q * mask_b], axis=0)
        s = lax.dot_general(qs, k, (((1,), (1,)), ((), ())), preferred_element_type=f32) + bias
        mx = jnp.max(s, axis=-1, keepdims=True)
        p = jnp.exp2((s - mx).astype(bf16))
        pv = jnp.dot(p, jnp.concatenate([v, ones], axis=1), preferred_element_type=f32)
        o = jnp.where(is_a, pv[:TQ, :LANES], pv[TQ:, :LANES])
        lrep = jnp.where(is_a, pv[:TQ, LANES:], pv[TQ:, LANES:])
        mrep = jnp.where(is_a, mx[:TQ], mx[TQ:])
        return mrep, lrep, o

    def merge(old, new):
        (m_old, l_old, acc_old), (mr, lr, o) = old, new
        m_new = jnp.maximum(m_old, mr)
        alpha = jnp.exp2(m_old - m_new)
        beta = jnp.exp2(mr - m_new)
        return m_new, l_old * alpha + lr * beta, acc_old * alpha + o * beta

    def rows(stat, lo, n):
        return tuple(x[lo:lo + n] for x in stat)

    def cat(stats):
        return tuple(jnp.concatenate(xs, axis=0) for xs in zip(*stats))

    for sub in range(SUBTILES):
        base = sub * TQ
        m0 = (pl.program_id(2) * SUBTILES + sub) * TQ

        st1 = [[None] * (TQ // 8) for _ in range(RES)]
        for jj2 in range(TQ // 16):
            qf = q_ref[0, 0, :, base + 16 * jj2:base + 16 * jj2 + 16, :].astype(f32)
            for par in range(2):
                jj = 2 * jj2 + par
                q = qf[:, 8 * par:8 * par + 8, :].reshape(TQ, LANES).astype(bf16)
                t0 = RES * (m0 + 8 * jj)
                ks = pl.multiple_of(jnp.clip(t0 - 64, 0, S - TK), 64)
                stat = tile(q, kn_ref[0, 0, pl.ds(ks, TK), :], vn_ref[0, 0, pl.ds(ks, TK), :],
                            bias_ref[(t0 - ks) // 64, 0])
                for r in range(RES):
                    st1[r][jj] = rows(stat, 8 * r, 8)

        st4 = [[None] * (TQ // 32) for _ in range(RES)]
        for b in range(4):
            for j4 in range(TQ // 32):
                ml = base + 32 * j4
                mg = m0 + 32 * j4
                ks = pl.multiple_of(jnp.clip(mg - 16, 0, M - 64), 16)
                q = jnp.concatenate([q_ref[0, 0, 4 * a + b, ml:ml + 32, :] for a in range(4)], axis=0)
                k = jnp.concatenate([k16_ref[0, 0, 4 * a + b, pl.ds(ks, 64), :] for a in range(4)], axis=0)
                v = jnp.concatenate([v16_ref[0, 0, 4 * a + b, pl.ds(ks, 64), :] for a in range(4)], axis=0)
                old = cat([st1[4 * a + b][4 * j4 + c] for a in range(4) for c in range(4)])
                stat = merge(old, tile(q, k, v, bias_ref[3 + (mg - ks) // 16, 0]))
                for a in range(4):
                    st4[4 * a + b][j4] = rows(stat, 32 * a, 32)

        ks = pl.multiple_of(jnp.clip(m0 - 64, 0, M - TK), 64)
        for r in range(RES):
            new = tile(q_ref[0, 0, r, base:base + TQ, :], k16_ref[0, 0, r, pl.ds(ks, TK), :],
                       v16_ref[0, 0, r, pl.ds(ks, TK), :], bias_ref[6 + (m0 - ks) // 64, 0])
            _, l_new, acc_new = merge(cat(st4[r]), new)
            gate = ga_ref[0, 0, r, base:base + TQ, :].astype(f32)
            o_ref[0, 0, r, base:base + TQ, :] = (acc_new / l_new * gate).astype(bf16)


def _attention(q16, k16, v16, kn, vn, ga16, bias, headmask):
    B, _, _, M, _ = q16.shape
    S = M * RES
    rows = SUBTILES * TQ
    tile_spec = pl.BlockSpec((1, 1, RES, rows, LANES), lambda b, j, t: (b, j, 0, t, 0))
    seq16_spec = pl.BlockSpec((1, 1, RES, M, LANES), lambda b, j, t: (b, j, 0, 0, 0))
    nat_spec = pl.BlockSpec((1, 1, S, LANES), lambda b, j, t: (b, j, 0, 0))
    return pl.pallas_call(
        functools.partial(_attn_kernel, M=M, S=S),
        grid=(B, N_PAIRS, M // rows),
        in_specs=[tile_spec, seq16_spec, seq16_spec, nat_spec, nat_spec, tile_spec,
                  pl.BlockSpec((9, 1, 2 * TQ, TK), lambda b, j, t: (0, j, 0, 0)),
                  pl.BlockSpec((2, LANES), lambda b, j, t: (0, 0))],
        out_specs=tile_spec,
        out_shape=jax.ShapeDtypeStruct((B, N_PAIRS, RES, M, LANES), bf16),
        compiler_params=pltpu.CompilerParams(
            dimension_semantics=("parallel", "parallel", "arbitrary"), vmem_limit_bytes=VMEM_LIMIT),
        name="attn",
    )(q16, k16, v16, kn, vn, ga16, bias, headmask)


def _cmul_const(z, k, n):
    re, im = z
    if k == 0:
        return z
    if 4 * k == n:
        return -im, re
    c = math.cos(2.0 * math.pi * k / n)
    s = math.sin(2.0 * math.pi * k / n)
    return re * c - im * s, im * c + re * s


def _fft_list(xs):
    n = len(xs)
    if n == 1:
        return xs
    even = _fft_list(xs[0::2])
    odd = _fft_list(xs[1::2])
    out = [None] * n
    for k in range(n // 2):
        t = _cmul_const(odd[k], k, n)
        out[k] = (even[k][0] + t[0], even[k][1] + t[1])
        out[k + n // 2] = (even[k][0] - t[0], even[k][1] - t[1])
    return out


def _fnet_kernel(p_ref, q_ref, gf_ref, tw_ref, d_ref, o_ref, br_ref, bi_ref, *, M):
    rb = 16

    def butterfly_body(i, carry):
        row = pl.multiple_of(i * rb, rb)
        tw = tw_ref[pl.ds(row, rb), :]
        for half in range(D_FOURIER // LANES):
            lanes = slice(half * LANES, (half + 1) * LANES)
            zs = []
            for s2 in range(RES):
                rows = pl.ds(pl.multiple_of(s2 * M + row, rb), rb)
                zs.append((p_ref[0, rows, lanes].astype(f32), q_ref[0, rows, lanes].astype(f32)))
            ys = _fft_list(zs)
            for k2 in range(RES):
                re, im = ys[k2]
                if k2:
                    c = tw[:, k2:k2 + 1]
                    s = tw[:, RES + k2:RES + k2 + 1]
                    re, im = re * c - im * s, im * c + re * s
                br_ref[k2, pl.ds(row, rb), lanes] = re.astype(bf16)
                bi_ref[k2, pl.ds(row, rb), lanes] = im.astype(bf16)
        return carry

    lax.fori_loop(0, M // rb, butterfly_body, 0)

    def dft_body(i, carry):
        for u in range(DFT_UNROLL):
            k2 = i * DFT_UNROLL + u
            rhs = jnp.concatenate([br_ref[k2], bi_ref[k2]], axis=0)
            res = jnp.dot(d_ref[...], rhs, preferred_element_type=f32)
            o_ref[0, k2] = (res * gf_ref[0, k2].astype(f32)).astype(bf16)
        return carry

    lax.fori_loop(0, RES // DFT_UNROLL, dft_body, 0)


def _fourier(p, q, gf16, tw, dmat):
    B, S, _ = p.shape
    M = S // RES
    seq_spec = pl.BlockSpec((1, S, D_FOURIER), lambda b: (b, 0, 0))
    res_spec = pl.BlockSpec((1, RES, M, D_FOURIER), lambda b: (b, 0, 0, 0))
    return pl.pallas_call(
        functools.partial(_fnet_kernel, M=M),
        grid=(B,),
        in_specs=[seq_spec, seq_spec, res_spec,
                  pl.BlockSpec((M, LANES), lambda b: (0, 0)),
                  pl.BlockSpec((M, 2 * M), lambda b: (0, 0))],
        out_specs=res_spec,
        out_shape=jax.ShapeDtypeStruct((B, RES, M, D_FOURIER), bf16),
        scratch_shapes=[pltpu.VMEM((RES, M, D_FOURIER), bf16)] * 2,
        compiler_params=pltpu.CompilerParams(
            dimension_semantics=("parallel",), vmem_limit_bytes=VMEM_LIMIT),
        name="fnet",
    )(p, q, gf16, tw, dmat)


def _outproj_kernel(a_ref, f_ref, x_ref, w_ref, y_ref, scr_ref, *, tmm):
    tm = RES * tmm
    attn = jnp.concatenate([a_ref[0, j].reshape(tm, LANES) for j in range(N_PAIRS)], axis=-1)
    mix = jnp.concatenate([attn, f_ref[0].reshape(tm, D_FOURIER)], axis=-1)
    y = jnp.dot(mix, w_ref[...], preferred_element_type=f32)
    for c in range(D_MODEL // LANES):
        for r in range(RES):
            scr_ref[c, pl.ds(r, tmm, stride=PITCH), :] = y[r * tmm:(r + 1) * tmm, c * LANES:(c + 1) * LANES]
    for c in range(D_MODEL // LANES):
        cols = slice(c * LANES, (c + 1) * LANES)
        for g in range(tmm):
            rows = slice(g * RES, (g + 1) * RES)
            y_ref[0, rows, cols] = x_ref[0, rows, cols] + scr_ref[c, g * PITCH:g * PITCH + RES, :]


def _outproj(attn16, four16, x, w_out, tmm=64):
    B, S, _ = x.shape
    M = S // RES
    tm = RES * tmm
    x_spec = pl.BlockSpec((1, tm, D_MODEL), lambda b, i: (b, i, 0))
    return pl.pallas_call(
        functools.partial(_outproj_kernel, tmm=tmm),
        grid=(B, M // tmm),
        in_specs=[pl.BlockSpec((1, N_PAIRS, RES, tmm, LANES), lambda b, i: (b, 0, 0, i, 0)),
                  pl.BlockSpec((1, RES, tmm, D_FOURIER), lambda b, i: (b, 0, i, 0)),
                  x_spec,
                  pl.BlockSpec((D_MODEL, D_MODEL), lambda b, i: (0, 0))],
        out_specs=x_spec,
        out_shape=jax.ShapeDtypeStruct((B, S, D_MODEL), f32),
        scratch_shapes=[pltpu.VMEM((D_MODEL // LANES, tmm * PITCH, LANES), f32)],
        compiler_params=pltpu.CompilerParams(
            dimension_semantics=("parallel", "parallel"), vmem_limit_bytes=VMEM_LIMIT),
        name="outproj",
    )(attn16, four16, x, w_out)


def _encoder_layer(x, norm_g, w_ext, qg, kg, bd, bias, headmask, w_out):
    tw, dmat = _dft_tables(x.shape[1])
    q16, k16, v16, ga16, gf16, kn, vn, p, q = _inproj(x, norm_g, w_ext, qg, kg, bd)
    attn16 = _attention(q16, k16, v16, kn, vn, ga16, bias, headmask)
    four16 = _fourier(p, q, gf16, jnp.asarray(tw), jnp.asarray(dmat, bf16))
    return _outproj(attn16, four16, x, w_out)


def _layer_params(norm_g, w_in, q_norm_g, k_norm_g, rel_bias, w_four, w_out):
    bias = _bias_tiles(rel_bias)
    headmask = jnp.asarray(np.stack([np.arange(LANES) < HEAD_DIM, np.arange(LANES) >= HEAD_DIM]), bf16)
    bd = jnp.asarray(np.kron(np.eye(256 // HEAD_DIM), np.full((HEAD_DIM, HEAD_DIM), 1.0 / HEAD_DIM)), bf16)
    w_p, w_q = _fold_fourier_weights(w_in[:, 4 * D_ATTN:4 * D_ATTN + D_FOURIER], w_four)
    w_ext = jnp.concatenate(
        [w_in[:, :4 * D_ATTN], w_in[:, 4 * D_ATTN + D_FOURIER:], w_p, w_q], axis=1).astype(bf16)
    qg = jnp.tile(q_norm_g, N_HEADS)[None, :] * (LOG2E / math.sqrt(HEAD_DIM))
    kg = jnp.tile(k_norm_g, N_HEADS)[None, :]
    return (norm_g[None, :], w_ext, qg, kg, bd, bias, headmask, w_out.astype(bf16))


def kernel(x_prompt, x_sample, norm_g, w_in, q_norm_g, k_norm_g, rel_bias, w_four, w_out):
    y_prompt, y_sample = x_prompt, x_sample
    for layer in range(norm_g.shape[0]):
        args = _layer_params(norm_g[layer], w_in[layer], q_norm_g[layer], k_norm_g[layer], rel_bias,
                             w_four[layer], w_out[layer])
        y_prompt = _encoder_layer(y_prompt, *args)
        y_sample = _encoder_layer(y_sample, *args)
    return (y_prompt, y_sample)
```

```python
import functools
import math

import numpy as np
import jax
import jax.numpy as jnp
from jax import lax
from jax.experimental import pallas as pl
from jax.experimental.pallas import tpu as pltpu

D_MODEL = 1024
HEAD_DIM = 64
N_HEADS = 12
N_PAIRS = N_HEADS // 2
D_ATTN = N_HEADS * HEAD_DIM
N_GROUPS = 4
GROUP = 64
D_FOURIER = N_GROUPS * GROUP
N_BUCKETS = 32
MAX_DISTANCE = 1024
RMS_EPS = 1e-6
NEG = -1e30
RES = 16
HALF = 64
TQ = 128
TK = 256
MAX_SUBTILES = 4
DFT_UNROLL = 4
LOG2E = math.log2(math.e)
LANES = 128
PITCH = 24
VMEM_LIMIT = 52 * 1024 * 1024

C_Q, C_K, C_V, C_GA, C_GF, C_P, C_QQ, C_END = 0, 768, 1536, 2304, 3072, 3328, 3584, 3840

f32 = jnp.float32
bf16 = jnp.bfloat16


def _t5_bucket_np(rel):
    half = N_BUCKETS // 2
    max_exact = half // 2
    ret = np.where(rel > 0, half, 0)
    n = np.abs(rel)
    nf = np.maximum(n, 1).astype(np.float32)
    large = max_exact + (np.log(nf / np.float32(max_exact)) / np.float32(math.log(MAX_DISTANCE / max_exact))
                         * np.float32(half - max_exact)).astype(np.int32)
    large = np.minimum(large, half - 1)
    return ret + np.where(n < max_exact, n, large)


def _bucket_tiles():
    tiles = np.zeros((3, 3, TQ, TK), np.int32)
    i = np.arange(TQ)[:, None]
    c = np.arange(TK)[None, :]
    for var in range(3):
        rel = c - 64 * var - (16 * (i % 8) + i // 8)
        tiles[0, var] = np.where(np.abs(rel) <= HALF, _t5_bucket_np(rel * 1), -1)
        rel = 4 * ((c % 64) - (i % 32) - 16 * var) + (c // 64 - i // 32)
        tiles[1, var] = np.where(np.abs(rel) <= HALF, _t5_bucket_np(rel * 4), -1)
        rel = c - 64 * var - i
        tiles[2, var] = np.where(np.abs(rel) <= HALF, _t5_bucket_np(rel * 16), -1)
    return tiles.reshape(9, TQ, TK)


def _dft_tables(seq):
    m = seq // RES
    s1 = np.arange(m)[:, None]
    k2 = np.arange(RES)[None, :]
    ang = 2.0 * np.pi * ((s1 * k2) % seq) / seq
    tw = np.zeros((m, LANES), np.float32)
    tw[:, :RES] = np.cos(ang)
    tw[:, RES:2 * RES] = np.sin(ang)
    k1 = np.arange(m)[:, None]
    ang = 2.0 * np.pi * ((k1 * np.arange(m)[None, :]) % m) / m
    scale = 1.0 / math.sqrt(seq)
    dmat = np.concatenate([np.cos(ang) * scale, -np.sin(ang) * scale], axis=1)
    return tw, dmat.astype(np.float32)


def _channel_dft_blockdiag():
    c = np.arange(GROUP)
    ang = 2.0 * np.pi * ((c[:, None] * c[None, :]) % GROUP) / GROUP
    eye = np.eye(N_GROUPS)
    scale = 1.0 / math.sqrt(GROUP)
    return (np.kron(eye, np.cos(ang) * scale).astype(np.float32),
            np.kron(eye, np.sin(ang) * scale).astype(np.float32))


def _bias_kernel(rb_ref, idx_ref, out_ref, *, present):
    for t in range(len(present)):
        idx = idx_ref[t]
        hits = [(b, idx == b) for b in present[t]]
        for head in range(N_HEADS):
            tile = jnp.full((TQ, TK), NEG, f32)
            for b, hit in hits:
                tile = jnp.where(hit, rb_ref[b, head] * LOG2E, tile)
            out_ref[t, head // 2, (head % 2) * TQ:(head % 2 + 1) * TQ, :] = tile


def _bias_tiles(rel_bias):
    idx = _bucket_tiles()
    present = tuple(tuple(int(b) for b in np.unique(tile) if b >= 0) for tile in idx)
    return pl.pallas_call(
        functools.partial(_bias_kernel, present=present),
        in_specs=[pl.BlockSpec(memory_space=pltpu.SMEM), pl.BlockSpec(memory_space=pltpu.VMEM)],
        out_specs=pl.BlockSpec(memory_space=pltpu.VMEM),
        out_shape=jax.ShapeDtypeStruct((9, N_PAIRS, 2 * TQ, TK), f32),
        compiler_params=pltpu.CompilerParams(vmem_limit_bytes=VMEM_LIMIT),
        name="bias_tiles",
    )(rel_bias, jnp.asarray(idx))


def _fold_kernel(wu_ref, ww_ref, cbd_ref, sbd_ref, wp_ref, wq_ref):
    hi = lax.Precision.HIGHEST
    gp = jnp.dot(cbd_ref[...], ww_ref[...], precision=hi, preferred_element_type=f32)
    gq = jnp.dot(sbd_ref[...], ww_ref[...], precision=hi, preferred_element_type=f32)
    wp_ref[...] = jnp.dot(wu_ref[...], gp, precision=hi, preferred_element_type=f32)
    wq_ref[...] = jnp.dot(wu_ref[...], gq, precision=hi, preferred_element_type=f32)


def _fold_fourier_weights(w_u, w_four):
    cbd, sbd = _channel_dft_blockdiag()
    blockmask = jnp.asarray(np.kron(np.eye(N_GROUPS), np.ones((GROUP, GROUP))).astype(np.float32))
    wwide = jnp.tile(w_four.reshape(D_FOURIER, GROUP), (1, N_GROUPS)) * blockmask
    return pl.pallas_call(
        _fold_kernel,
        out_shape=(jax.ShapeDtypeStruct((D_MODEL, D_FOURIER), f32),
                   jax.ShapeDtypeStruct((D_MODEL, D_FOURIER), f32)),
        name="fold_fourier",
    )(w_u, wwide, jnp.asarray(cbd), jnp.asarray(sbd))


def _silu(x):
    return x * (1.0 / (1.0 + jnp.exp(-x)))


def _inproj_kernel(x_ref, g_ref, w_ref, qg_ref, kg_ref, bd_ref,
                   q16_ref, k16_ref, v16_ref, ga16_ref, gf16_ref, kn_ref, vn_ref, p_ref, qq_ref,
                   scr_ref, *, tmm):
    xs = x_ref[0]
    ms = jnp.mean(xs * xs, axis=-1, keepdims=True)
    h = (xs * lax.rsqrt(ms + RMS_EPS) * g_ref[...]).astype(bf16)

    def proj(lo, hi):
        return jnp.dot(h, w_ref[:, lo:hi], preferred_element_type=f32)

    def head_norm(t, gain):
        t2 = (t * t).astype(bf16)
        msq = jnp.concatenate(
            [jnp.dot(t2[:, c * 256:(c + 1) * 256], bd_ref[...], preferred_element_type=f32)
             for c in range(D_ATTN // 256)], axis=-1)
        return t * lax.rsqrt(msq + RMS_EPS) * gain

    def residue_rows(slab, val):
        for g in range(tmm):
            scr_ref[slab, g * PITCH:g * PITCH + RES, :] = val[g * RES:(g + 1) * RES]
        return [scr_ref[slab, pl.ds(r, tmm, stride=PITCH), :] for r in range(RES)]

    def put_res(ref, val, first_slab):
        for j in range(N_PAIRS):
            rows = residue_rows(first_slab + j, val[:, j * LANES:(j + 1) * LANES])
            for r in range(RES):
                ref[0, j, r] = rows[r].astype(bf16)

    def put_nat(ref, val):
        for j in range(N_PAIRS):
            ref[0, j] = val[:, j * LANES:(j + 1) * LANES].astype(bf16)

    qn = head_norm(proj(C_Q, C_K), qg_ref[...])
    put_res(q16_ref, qn, 0)
    kn = head_norm(proj(C_K, C_V), kg_ref[...])
    put_res(k16_ref, kn, 6)
    put_nat(kn_ref, kn)
    v = proj(C_V, C_GA)
    put_res(v16_ref, v, 12)
    put_nat(vn_ref, v)
    put_res(ga16_ref, _silu(proj(C_GA, C_GF)), 18)
    half = h.shape[0] // 2
    four = jnp.concatenate(
        [jnp.dot(h[:half], w_ref[:, C_GF:C_END], preferred_element_type=f32),
         jnp.dot(h[half:], w_ref[:, C_GF:C_END], preferred_element_type=f32)], axis=0)
    gf = _silu(four[:, :D_FOURIER])
    for c in range(D_FOURIER // LANES):
        rows = residue_rows(24 + c, gf[:, c * LANES:(c + 1) * LANES])
        for r in range(RES):
            gf16_ref[0, r, :, c * LANES:(c + 1) * LANES] = rows[r].astype(bf16)
    p_ref[0] = four[:, C_P - C_GF:C_QQ - C_GF].astype(bf16)
    qq_ref[0] = four[:, C_QQ - C_GF:].astype(bf16)


def _inproj(x, norm_g, w_ext, qg, kg, bd, tmm=32):
    B, S, _ = x.shape
    M = S // RES
    tm = RES * tmm
    res_shape = jax.ShapeDtypeStruct((B, N_PAIRS, RES, M, LANES), bf16)
    nat_shape = jax.ShapeDtypeStruct((B, N_PAIRS, S, LANES), bf16)
    four_shape = jax.ShapeDtypeStruct((B, S, D_FOURIER), bf16)
    res_spec = pl.BlockSpec((1, N_PAIRS, RES, tmm, LANES), lambda b, i: (b, 0, 0, i, 0))
    nat_spec = pl.BlockSpec((1, N_PAIRS, tm, LANES), lambda b, i: (b, 0, i, 0))
    four_spec = pl.BlockSpec((1, tm, D_FOURIER), lambda b, i: (b, i, 0))
    const = lambda shape: pl.BlockSpec(shape, lambda b, i: (0,) * len(shape))
    n_slabs = (4 * D_ATTN + D_FOURIER) // LANES
    return pl.pallas_call(
        functools.partial(_inproj_kernel, tmm=tmm),
        grid=(B, M // tmm),
        in_specs=[pl.BlockSpec((1, tm, D_MODEL), lambda b, i: (b, i, 0)),
                  const((1, D_MODEL)), const((D_MODEL, C_END)),
                  const((1, D_ATTN)), const((1, D_ATTN)), const((256, 256))],
        out_specs=[res_spec, res_spec, res_spec, res_spec,
                   pl.BlockSpec((1, RES, tmm, D_FOURIER), lambda b, i: (b, 0, i, 0)),
                   nat_spec, nat_spec, four_spec, four_spec],
        out_shape=[res_shape, res_shape, res_shape, res_shape,
                   jax.ShapeDtypeStruct((B, RES, M, D_FOURIER), bf16),
                   nat_shape, nat_shape, four_shape, four_shape],
        scratch_shapes=[pltpu.VMEM((n_slabs, tmm * PITCH, LANES), f32)],
        compiler_params=pltpu.CompilerParams(
            dimension_semantics=("parallel", "parallel"), vmem_limit_bytes=VMEM_LIMIT),
        name="inproj",
    )(x, norm_g, w_ext, qg, kg, bd)


def _attn_kernel(q_ref, k16_ref, v16_ref, kn_ref, vn_ref, ga_ref, bias_ref, hm_ref, o_ref, *, M, S, nsub):
    is_a = lax.broadcasted_iota(jnp.int32, (TQ, LANES), 1) < HEAD_DIM
    mask_a = hm_ref[0:1, :]
    mask_b = hm_ref[1:2, :]
    ones = jnp.ones((TK, LANES), bf16)

    def tile(q, k, v, bias):
        qs = jnp.concatenate([q * mask_a, q * mask_b], axis=0)
        s = lax.dot_general(qs, k, (((1,), (1,)), ((), ())), preferred_element_type=f32) + bias
        mx = jnp.max(s, axis=-1, keepdims=True)
        p = jnp.exp2((s - mx).astype(bf16))
        pv = jnp.dot(p, jnp.concatenate([v, ones], axis=1), preferred_element_type=f32)
        o = jnp.where(is_a, pv[:TQ, :LANES], pv[TQ:, :LANES])
        lrep = jnp.where(is_a, pv[:TQ, LANES:], pv[TQ:, LANES:])
        mrep = jnp.where(is_a, mx[:TQ], mx[TQ:])
        return mrep, lrep, o

    def merge(old, new):
        (m_old, l_old, acc_old), (mr, lr, o) = old, new
        m_new = jnp.maximum(m_old, mr)
        alpha = jnp.exp2(m_old - m_new)
        beta = jnp.exp2(mr - m_new)
        return m_new, l_old * alpha + lr * beta, acc_old * alpha + o * beta

    def rows(stat, lo, n):
        return tuple(x[lo:lo + n] for x in stat)

    def cat(stats):
        return tuple(jnp.concatenate(xs, axis=0) for xs in zip(*stats))

    for sub in range(nsub):
        base = sub * TQ
        m0 = (pl.program_id(2) * nsub + sub) * TQ

        st1 = [[None] * (TQ // 8) for _ in range(RES)]
        for jj2 in range(TQ // 16):
            qf = q_ref[0, 0, :, base + 16 * jj2:base + 16 * jj2 + 16, :].astype(f32)
            for par in range(2):
                jj = 2 * jj2 + par
                q = qf[:, 8 * par:8 * par + 8, :].reshape(TQ, LANES).astype(bf16)
                t0 = RES * (m0 + 8 * jj)
                ks = pl.multiple_of(jnp.clip(t0 - 64, 0, S - TK), 64)
                stat = tile(q, kn_ref[0, 0, pl.ds(ks, TK), :], vn_ref[0, 0, pl.ds(ks, TK), :],
                            bias_ref[(t0 - ks) // 64, 0])
                for r in range(RES):
                    st1[r][jj] = rows(stat, 8 * r, 8)

        st4 = [[None] * (TQ // 32) for _ in range(RES)]
        for b in range(4):
            for j4 in range(TQ // 32):
                ml = base + 32 * j4
                mg = m0 + 32 * j4
                ks = pl.multiple_of(jnp.clip(mg - 16, 0, M - 64), 16)
                q = jnp.concatenate([q_ref[0, 0, 4 * a + b, ml:ml + 32, :] for a in range(4)], axis=0)
                k = jnp.concatenate([k16_ref[0, 0, 4 * a + b, pl.ds(ks, 64), :] for a in range(4)], axis=0)
                v = jnp.concatenate([v16_ref[0, 0, 4 * a + b, pl.ds(ks, 64), :] for a in range(4)], axis=0)
                old = cat([st1[4 * a + b][4 * j4 + c] for a in range(4) for c in range(4)])
                stat = merge(old, tile(q, k, v, bias_ref[3 + (mg - ks) // 16, 0]))
                for a in range(4):
                    st4[4 * a + b][j4] = rows(stat, 32 * a, 32)

        ks = pl.multiple_of(jnp.clip(m0 - 64, 0, M - TK), 64)
        for r in range(RES):
            new = tile(q_ref[0, 0, r, base:base + TQ, :], k16_ref[0, 0, r, pl.ds(ks, TK), :],
                       v16_ref[0, 0, r, pl.ds(ks, TK), :], bias_ref[6 + (m0 - ks) // 64, 0])
            _, l_new, acc_new = merge(cat(st4[r]), new)
            gate = ga_ref[0, 0, r, base:base + TQ, :].astype(f32)
            o_ref[0, 0, r, base:base + TQ, :] = (acc_new / l_new * gate).astype(bf16)


def _attention(q16, k16, v16, kn, vn, ga16, bias, headmask):
    B, _, _, M, _ = q16.shape
    S = M * RES
    nsub = min(MAX_SUBTILES, M // TQ)
    rows = nsub * TQ
    tile_spec = pl.BlockSpec((1, 1, RES, rows, LANES), lambda b, j, t: (b, j, 0, t, 0))
    seq16_spec = pl.BlockSpec((1, 1, RES, M, LANES), lambda b, j, t: (b, j, 0, 0, 0))
    nat_spec = pl.BlockSpec((1, 1, S, LANES), lambda b, j, t: (b, j, 0, 0))
    return pl.pallas_call(
        functools.partial(_attn_kernel, M=M, S=S, nsub=nsub),
        grid=(B, N_PAIRS, M // rows),
        in_specs=[tile_spec, seq16_spec, seq16_spec, nat_spec, nat_spec, tile_spec,
                  pl.BlockSpec((9, 1, 2 * TQ, TK), lambda b, j, t: (0, j, 0, 0)),
                  pl.BlockSpec((2, LANES), lambda b, j, t: (0, 0))],
        out_specs=tile_spec,
        out_shape=jax.ShapeDtypeStruct((B, N_PAIRS, RES, M, LANES), bf16),
        compiler_params=pltpu.CompilerParams(
            dimension_semantics=("parallel", "parallel", "arbitrary"), vmem_limit_bytes=VMEM_LIMIT),
        name="attn",
    )(q16, k16, v16, kn, vn, ga16, bias, headmask)


def _cmul_const(z, k, n):
    re, im = z
    if k == 0:
        return z
    if 4 * k == n:
        return -im, re
    c = math.cos(2.0 * math.pi * k / n)
    s = math.sin(2.0 * math.pi * k / n)
    return re * c - im * s, im * c + re * s


def _fft_list(xs):
    n = len(xs)
    if n == 1:
        return xs
    even = _fft_list(xs[0::2])
    odd = _fft_list(xs[1::2])
    out = [None] * n
    for k in range(n // 2):
        t = _cmul_const(odd[k], k, n)
        out[k] = (even[k][0] + t[0], even[k][1] + t[1])
        out[k + n // 2] = (even[k][0] - t[0], even[k][1] - t[1])
    return out


def _fnet_kernel(p_ref, q_ref, gf_ref, tw_ref, d_ref, o_ref, br_ref, bi_ref, *, M):
    rb = 16

    def butterfly_body(i, carry):
        row = pl.multiple_of(i * rb, rb)
        tw = tw_ref[pl.ds(row, rb), :]
        for half in range(D_FOURIER // LANES):
            lanes = slice(half * LANES, (half + 1) * LANES)
            zs = []
            for s2 in range(RES):
                rows = pl.ds(pl.multiple_of(s2 * M + row, rb), rb)
                zs.append((p_ref[0, rows, lanes].astype(f32), q_ref[0, rows, lanes].astype(f32)))
            ys = _fft_list(zs)
            for k2 in range(RES):
                re, im = ys[k2]
                if k2:
                    c = tw[:, k2:k2 + 1]
                    s = tw[:, RES + k2:RES + k2 + 1]
                    re, im = re * c - im * s, im * c + re * s
                br_ref[k2, pl.ds(row, rb), lanes] = re.astype(bf16)
                bi_ref[k2, pl.ds(row, rb), lanes] = im.astype(bf16)
        return carry

    lax.fori_loop(0, M // rb, butterfly_body, 0)

    def dft_body(i, carry):
        for u in range(DFT_UNROLL):
            k2 = i * DFT_UNROLL + u
            rhs = jnp.concatenate([br_ref[k2], bi_ref[k2]], axis=0)
            res = jnp.dot(d_ref[...], rhs, preferred_element_type=f32)
            o_ref[0, k2] = (res * gf_ref[0, k2].astype(f32)).astype(bf16)
        return carry

    lax.fori_loop(0, RES // DFT_UNROLL, dft_body, 0)


def _fourier(p, q, gf16, tw, dmat):
    B, S, _ = p.shape
    M = S // RES
    seq_spec = pl.BlockSpec((1, S, D_FOURIER), lambda b: (b, 0, 0))
    res_spec = pl.BlockSpec((1, RES, M, D_FOURIER), lambda b: (b, 0, 0, 0))
    return pl.pallas_call(
        functools.partial(_fnet_kernel, M=M),
        grid=(B,),
        in_specs=[seq_spec, seq_spec, res_spec,
                  pl.BlockSpec((M, LANES), lambda b: (0, 0)),
                  pl.BlockSpec((M, 2 * M), lambda b: (0, 0))],
        out_specs=res_spec,
        out_shape=jax.ShapeDtypeStruct((B, RES, M, D_FOURIER), bf16),
        scratch_shapes=[pltpu.VMEM((RES, M, D_FOURIER), bf16)] * 2,
        compiler_params=pltpu.CompilerParams(
            dimension_semantics=("parallel",), vmem_limit_bytes=VMEM_LIMIT),
        name="fnet",
    )(p, q, gf16, tw, dmat)


def _outproj_kernel(a_ref, f_ref, x_ref, w_ref, y_ref, scr_ref, *, tmm):
    tm = RES * tmm
    attn = jnp.concatenate([a_ref[0, j].reshape(tm, LANES) for j in range(N_PAIRS)], axis=-1)
    mix = jnp.concatenate([attn, f_ref[0].reshape(tm, D_FOURIER)], axis=-1)
    y = jnp.dot(mix, w_ref[...], preferred_element_type=f32)
    for c in range(D_MODEL // LANES):
        for r in range(RES):
            scr_ref[c, pl.ds(r, tmm, stride=PITCH), :] = y[r * tmm:(r + 1) * tmm, c * LANES:(c + 1) * LANES]
    for c in range(D_MODEL // LANES):
        cols = slice(c * LANES, (c + 1) * LANES)
        for g in range(tmm):
            rows = slice(g * RES, (g + 1) * RES)
            y_ref[0, rows, cols] = x_ref[0, rows, cols] + scr_ref[c, g * PITCH:g * PITCH + RES, :]


def _outproj(attn16, four16, x, w_out, tmm=64):
    B, S, _ = x.shape
    M = S // RES
    tm = RES * tmm
    x_spec = pl.BlockSpec((1, tm, D_MODEL), lambda b, i: (b, i, 0))
    return pl.pallas_call(
        functools.partial(_outproj_kernel, tmm=tmm),
        grid=(B, M // tmm),
        in_specs=[pl.BlockSpec((1, N_PAIRS, RES, tmm, LANES), lambda b, i: (b, 0, 0, i, 0)),
                  pl.BlockSpec((1, RES, tmm, D_FOURIER), lambda b, i: (b, 0, i, 0)),
                  x_spec,
                  pl.BlockSpec((D_MODEL, D_MODEL), lambda b, i: (0, 0))],
        out_specs=x_spec,
        out_shape=jax.ShapeDtypeStruct((B, S, D_MODEL), f32),
        scratch_shapes=[pltpu.VMEM((D_MODEL // LANES, tmm * PITCH, LANES), f32)],
        compiler_params=pltpu.CompilerParams(
            dimension_semantics=("parallel", "parallel"), vmem_limit_bytes=VMEM_LIMIT),
        name="outproj",
    )(attn16, four16, x, w_out)


def _encoder_layer(x, norm_g, w_ext, qg, kg, bd, bias, headmask, w_out):
    tw, dmat = _dft_tables(x.shape[1])
    q16, k16, v16, ga16, gf16, kn, vn, p, q = _inproj(x, norm_g, w_ext, qg, kg, bd)
    attn16 = _attention(q16, k16, v16, kn, vn, ga16, bias, headmask)
    four16 = _fourier(p, q, gf16, jnp.asarray(tw), jnp.asarray(dmat, bf16))
    return _outproj(attn16, four16, x, w_out)


def _layer_params(norm_g, w_in, q_norm_g, k_norm_g, rel_bias, w_four, w_out):
    bias = _bias_tiles(rel_bias)
    headmask = jnp.asarray(np.stack([np.arange(LANES) < HEAD_DIM, np.arange(LANES) >= HEAD_DIM]), bf16)
    bd = jnp.asarray(np.kron(np.eye(256 // HEAD_DIM), np.full((HEAD_DIM, HEAD_DIM), 1.0 / HEAD_DIM)), bf16)
    w_p, w_q = _fold_fourier_weights(w_in[:, 4 * D_ATTN:4 * D_ATTN + D_FOURIER], w_four)
    w_ext = jnp.concatenate(
        [w_in[:, :4 * D_ATTN], w_in[:, 4 * D_ATTN + D_FOURIER:], w_p, w_q], axis=1).astype(bf16)
    qg = jnp.tile(q_norm_g, N_HEADS)[None, :] * (LOG2E / math.sqrt(HEAD_DIM))
    kg = jnp.tile(k_norm_g, N_HEADS)[None, :]
    return (norm_g[None, :], w_ext, qg, kg, bd, bias, headmask, w_out.astype(bf16))


def kernel(x_prompt, x_sample, norm_g, w_in, q_norm_g, k_norm_g, rel_bias, w_four, w_out):
    y_prompt, y_sample = x_prompt, x_sample
    for layer in range(norm_g.shape[0]):
        args = _layer_params(norm_g[layer], w_in[layer], q_norm_g[layer], k_norm_g[layer], rel_bias,
                             w_four[layer], w_out[layer])
        y_prompt = _encoder_layer(y_prompt, *args)
        y_sample = _encoder_layer(y_sample, *args)
    return (y_prompt, y_sample)
```

```python
import functools
import math

import numpy as np
import jax
import jax.numpy as jnp
from jax import lax
from jax.experimental import pallas as pl
from jax.experimental.pallas import tpu as pltpu

D_MODEL = 1024
HEAD_DIM = 64
N_HEADS = 12
N_PAIRS = N_HEADS // 2
D_ATTN = N_HEADS * HEAD_DIM
N_GROUPS = 4
GROUP = 64
D_FOURIER = N_GROUPS * GROUP
N_BUCKETS = 32
MAX_DISTANCE = 1024
RMS_EPS = 1e-6
NEG = -1e30
RES = 16
HALF = 64
TQ = 128
TK = 256
D4_RES = 4
MXU = 256
MAX_SUBTILES = 4
DFT_UNROLL = 8
LOG2E = math.log2(math.e)
LANES = 128
PITCH = 24
VMEM_LIMIT = 52 * 1024 * 1024

C_Q, C_K, C_V, C_GA, C_GF, C_P, C_QQ, C_END = 0, 768, 1536, 2304, 3072, 3328, 3584, 3840

f32 = jnp.float32
bf16 = jnp.bfloat16


def _t5_bucket_np(rel):
    half = N_BUCKETS // 2
    max_exact = half // 2
    ret = np.where(rel > 0, half, 0)
    n = np.abs(rel)
    nf = np.maximum(n, 1).astype(np.float32)
    large = max_exact + (np.log(nf / np.float32(max_exact)) / np.float32(math.log(MAX_DISTANCE / max_exact))
                         * np.float32(half - max_exact)).astype(np.int32)
    large = np.minimum(large, half - 1)
    return ret + np.where(n < max_exact, n, large)


def _bucket_tiles():
    tiles = np.zeros((3, 3, TQ, TK), np.int32)
    i = np.arange(TQ)[:, None]
    c = np.arange(TK)[None, :]
    for var in range(3):
        rel = c - HALF * var - (RES * (i % (TQ // RES)) + i // (TQ // RES))
        tiles[0, var] = np.where(np.abs(rel) <= HALF, _t5_bucket_np(rel * 1), -1)
        qr, kr = TQ // D4_RES, TK // D4_RES
        rel = D4_RES * ((c % kr) - (i % qr) - (HALF // D4_RES) * var) + (c // kr - i // qr)
        tiles[1, var] = np.where(np.abs(rel) <= HALF, _t5_bucket_np(rel * D4_RES), -1)
        rel = c - HALF * var - i
        tiles[2, var] = np.where(np.abs(rel) <= HALF, _t5_bucket_np(rel * RES), -1)
    return tiles.reshape(9, TQ, TK)


def _dft_tables(seq):
    m = seq // RES
    s1 = np.arange(m)[:, None]
    k2 = np.arange(RES)[None, :]
    ang = 2.0 * np.pi * ((s1 * k2) % seq) / seq
    tw = np.zeros((m, LANES), np.float32)
    tw[:, :RES] = np.cos(ang)
    tw[:, RES:2 * RES] = np.sin(ang)
    k1 = np.arange(m)[:, None]
    ang = 2.0 * np.pi * ((k1 * np.arange(m)[None, :]) % m) / m
    scale = 1.0 / math.sqrt(seq)
    dmat = np.concatenate([np.cos(ang) * scale, -np.sin(ang) * scale], axis=1)
    return tw, dmat.astype(np.float32)


def _channel_dft_blockdiag():
    c = np.arange(GROUP)
    ang = 2.0 * np.pi * ((c[:, None] * c[None, :]) % GROUP) / GROUP
    eye = np.eye(N_GROUPS)
    scale = 1.0 / math.sqrt(GROUP)
    return (np.kron(eye, np.cos(ang) * scale).astype(np.float32),
            np.kron(eye, np.sin(ang) * scale).astype(np.float32))


def _bias_kernel(rb_ref, idx_ref, out_ref, *, present):
    for t in range(len(present)):
        idx = idx_ref[t]
        hits = [(b, idx == b) for b in present[t]]
        for head in range(N_HEADS):
            tile = jnp.full((TQ, TK), NEG, f32)
            for b, hit in hits:
                tile = jnp.where(hit, rb_ref[b, head] * LOG2E, tile)
            out_ref[t, head // 2, (head % 2) * TQ:(head % 2 + 1) * TQ, :] = tile


def _bias_tiles(rel_bias):
    idx = _bucket_tiles()
    present = tuple(tuple(int(b) for b in np.unique(tile) if b >= 0) for tile in idx)
    return pl.pallas_call(
        functools.partial(_bias_kernel, present=present),
        in_specs=[pl.BlockSpec(memory_space=pltpu.SMEM), pl.BlockSpec(memory_space=pltpu.VMEM)],
        out_specs=pl.BlockSpec(memory_space=pltpu.VMEM),
        out_shape=jax.ShapeDtypeStruct((9, N_PAIRS, 2 * TQ, TK), f32),
        compiler_params=pltpu.CompilerParams(vmem_limit_bytes=VMEM_LIMIT),
        name="bias_tiles",
    )(rel_bias, jnp.asarray(idx))


def _fold_kernel(wu_ref, ww_ref, cbd_ref, sbd_ref, wp_ref, wq_ref):
    hi = lax.Precision.HIGHEST
    gp = jnp.dot(cbd_ref[...], ww_ref[...], precision=hi, preferred_element_type=f32)
    gq = jnp.dot(sbd_ref[...], ww_ref[...], precision=hi, preferred_element_type=f32)
    wp_ref[...] = jnp.dot(wu_ref[...], gp, precision=hi, preferred_element_type=f32)
    wq_ref[...] = jnp.dot(wu_ref[...], gq, precision=hi, preferred_element_type=f32)


def _fold_fourier_weights(w_u, w_four):
    cbd, sbd = _channel_dft_blockdiag()
    blockmask = jnp.asarray(np.kron(np.eye(N_GROUPS), np.ones((GROUP, GROUP))).astype(np.float32))
    wwide = jnp.tile(w_four.reshape(D_FOURIER, GROUP), (1, N_GROUPS)) * blockmask
    return pl.pallas_call(
        _fold_kernel,
        out_shape=(jax.ShapeDtypeStruct((D_MODEL, D_FOURIER), f32),
                   jax.ShapeDtypeStruct((D_MODEL, D_FOURIER), f32)),
        name="fold_fourier",
    )(w_u, wwide, jnp.asarray(cbd), jnp.asarray(sbd))


def _silu(x):
    return x * (1.0 / (1.0 + jnp.exp(-x)))


def _inproj_kernel(x_ref, g_ref, w_ref, qg_ref, kg_ref, bd_ref,
                   q16_ref, k16_ref, v16_ref, ga16_ref, gf16_ref, kn_ref, vn_ref, p_ref, qq_ref,
                   scr_ref, *, tmm):
    xs = x_ref[0]
    ms = jnp.mean(xs * xs, axis=-1, keepdims=True)
    h = (xs * lax.rsqrt(ms + RMS_EPS) * g_ref[...]).astype(bf16)

    def proj(lo, hi):
        return jnp.dot(h, w_ref[:, lo:hi], preferred_element_type=f32)

    def head_norm(t, gain):
        t2 = (t * t).astype(bf16)
        msq = jnp.concatenate(
            [jnp.dot(t2[:, c * MXU:(c + 1) * MXU], bd_ref[...], preferred_element_type=f32)
             for c in range(D_ATTN // MXU)], axis=-1)
        return t * lax.rsqrt(msq + RMS_EPS) * gain

    def residue_rows(slab, val):
        for g in range(tmm):
            scr_ref[slab, g * PITCH:g * PITCH + RES, :] = val[g * RES:(g + 1) * RES]
        return [scr_ref[slab, pl.ds(r, tmm, stride=PITCH), :] for r in range(RES)]

    def put_res(ref, val, first_slab):
        for j in range(N_PAIRS):
            rows = residue_rows(first_slab + j, val[:, j * LANES:(j + 1) * LANES])
            for r in range(RES):
                ref[0, j, r] = rows[r].astype(bf16)

    def put_nat(ref, val):
        for j in range(N_PAIRS):
            ref[0, j] = val[:, j * LANES:(j + 1) * LANES].astype(bf16)

    qn = head_norm(proj(C_Q, C_K), qg_ref[...])
    put_res(q16_ref, qn, 0)
    kn = head_norm(proj(C_K, C_V), kg_ref[...])
    put_res(k16_ref, kn, 6)
    put_nat(kn_ref, kn)
    v = proj(C_V, C_GA)
    put_res(v16_ref, v, 12)
    put_nat(vn_ref, v)
    put_res(ga16_ref, _silu(proj(C_GA, C_GF)), 18)
    half = h.shape[0] // 2
    four = jnp.concatenate(
        [jnp.dot(h[:half], w_ref[:, C_GF:C_END], preferred_element_type=f32),
         jnp.dot(h[half:], w_ref[:, C_GF:C_END], preferred_element_type=f32)], axis=0)
    gf = _silu(four[:, :D_FOURIER])
    for c in range(D_FOURIER // LANES):
        rows = residue_rows(24 + c, gf[:, c * LANES:(c + 1) * LANES])
        for r in range(RES):
            gf16_ref[0, r, :, c * LANES:(c + 1) * LANES] = rows[r].astype(bf16)
    p_ref[0] = four[:, C_P - C_GF:C_QQ - C_GF].astype(bf16)
    qq_ref[0] = four[:, C_QQ - C_GF:].astype(bf16)


def _inproj(x, norm_g, w_ext, qg, kg, bd, tmm=32):
    B, S, _ = x.shape
    M = S // RES
    tm = RES * tmm
    res_shape = jax.ShapeDtypeStruct((B, N_PAIRS, RES, M, LANES), bf16)
    nat_shape = jax.ShapeDtypeStruct((B, N_PAIRS, S, LANES), bf16)
    four_shape = jax.ShapeDtypeStruct((B, S, D_FOURIER), bf16)
    res_spec = pl.BlockSpec((1, N_PAIRS, RES, tmm, LANES), lambda b, i: (b, 0, 0, i, 0))
    nat_spec = pl.BlockSpec((1, N_PAIRS, tm, LANES), lambda b, i: (b, 0, i, 0))
    four_spec = pl.BlockSpec((1, tm, D_FOURIER), lambda b, i: (b, i, 0))
    const = lambda shape: pl.BlockSpec(shape, lambda b, i: (0,) * len(shape))
    n_slabs = (4 * D_ATTN + D_FOURIER) // LANES
    return pl.pallas_call(
        functools.partial(_inproj_kernel, tmm=tmm),
        grid=(B, M // tmm),
        in_specs=[pl.BlockSpec((1, tm, D_MODEL), lambda b, i: (b, i, 0)),
                  const((1, D_MODEL)), const((D_MODEL, C_END)),
                  const((1, D_ATTN)), const((1, D_ATTN)), const((MXU, MXU))],
        out_specs=[res_spec, res_spec, res_spec, res_spec,
                   pl.BlockSpec((1, RES, tmm, D_FOURIER), lambda b, i: (b, 0, i, 0)),
                   nat_spec, nat_spec, four_spec, four_spec],
        out_shape=[res_shape, res_shape, res_shape, res_shape,
                   jax.ShapeDtypeStruct((B, RES, M, D_FOURIER), bf16),
                   nat_shape, nat_shape, four_shape, four_shape],
        scratch_shapes=[pltpu.VMEM((n_slabs, tmm * PITCH, LANES), f32)],
        compiler_params=pltpu.CompilerParams(
            dimension_semantics=("parallel", "parallel"), vmem_limit_bytes=VMEM_LIMIT),
        name="inproj",
    )(x, norm_g, w_ext, qg, kg, bd)


def _attn_kernel(q_ref, k16_ref, v16_ref, kn_ref, vn_ref, ga_ref, bias_ref, hm_ref, o_ref, *, M, S, nsub):
    is_a = lax.broadcasted_iota(jnp.int32, (TQ, LANES), 1) < HEAD_DIM
    mask_a = hm_ref[0:1, :]
    mask_b = hm_ref[1:2, :]
    ones = jnp.ones((TK, LANES), bf16)

    def tile(q, k, v, bias):
        qs = jnp.concatenate([q * mask_a, q * mask_b], axis=0)
        s = lax.dot_general(qs, k, (((1,), (1,)), ((), ())), preferred_element_type=f32) + bias
        mx = jnp.max(s, axis=-1, keepdims=True)
        p = jnp.exp2((s - mx).astype(bf16))
        pv = jnp.dot(p, jnp.concatenate([v, ones], axis=1), preferred_element_type=f32)
        o = jnp.where(is_a, pv[:TQ, :LANES], pv[TQ:, :LANES])
        lrep = jnp.where(is_a, pv[:TQ, LANES:], pv[TQ:, LANES:])
        mrep = jnp.where(is_a, mx[:TQ], mx[TQ:])
        return mrep, lrep, o

    def merge(old, new):
        (m_old, l_old, acc_old), (mr, lr, o) = old, new
        m_new = jnp.maximum(m_old, mr)
        alpha = jnp.exp2(m_old - m_new)
        beta = jnp.exp2(mr - m_new)
        return m_new, l_old * alpha + lr * beta, acc_old * alpha + o * beta

    def rows(stat, lo, n):
        return tuple(x[lo:lo + n] for x in stat)

    def cat(stats):
        return tuple(jnp.concatenate(xs, axis=0) for xs in zip(*stats))

    for sub in range(nsub):
        base = sub * TQ
        m0 = (pl.program_id(2) * nsub + sub) * TQ

        st1 = [[None] * (TQ // 8) for _ in range(RES)]
        for jj2 in range(TQ // 16):
            qf = q_ref[0, 0, :, base + 16 * jj2:base + 16 * jj2 + 16, :].astype(f32)
            for par in range(2):
                jj = 2 * jj2 + par
                q = qf[:, 8 * par:8 * par + 8, :].reshape(TQ, LANES).astype(bf16)
                t0 = RES * (m0 + 8 * jj)
                ks = pl.multiple_of(jnp.clip(t0 - HALF, 0, S - TK), HALF)
                stat = tile(q, kn_ref[0, 0, pl.ds(ks, TK), :], vn_ref[0, 0, pl.ds(ks, TK), :],
                            bias_ref[(t0 - ks) // HALF, 0])
                for r in range(RES):
                    st1[r][jj] = rows(stat, 8 * r, 8)

        qr, kr, side = TQ // D4_RES, TK // D4_RES, HALF // D4_RES
        st4 = [[None] * D4_RES for _ in range(RES)]
        for b in range(D4_RES):
            for j4 in range(TQ // qr):
                ml = base + qr * j4
                mg = m0 + qr * j4
                ks = pl.multiple_of(jnp.clip(mg - side, 0, M - kr), side)
                res4 = [D4_RES * a + b for a in range(RES // D4_RES)]
                q = jnp.concatenate([q_ref[0, 0, r, ml:ml + qr, :] for r in res4], axis=0)
                k = jnp.concatenate([k16_ref[0, 0, r, pl.ds(ks, kr), :] for r in res4], axis=0)
                v = jnp.concatenate([v16_ref[0, 0, r, pl.ds(ks, kr), :] for r in res4], axis=0)
                old = cat([st1[r][(qr // 8) * j4 + c] for r in res4 for c in range(qr // 8)])
                stat = merge(old, tile(q, k, v, bias_ref[3 + (mg - ks) // side, 0]))
                for a, r in enumerate(res4):
                    st4[r][j4] = rows(stat, qr * a, qr)

        ks = pl.multiple_of(jnp.clip(m0 - HALF, 0, M - TK), HALF)
        for r in range(RES):
            new = tile(q_ref[0, 0, r, base:base + TQ, :], k16_ref[0, 0, r, pl.ds(ks, TK), :],
                       v16_ref[0, 0, r, pl.ds(ks, TK), :], bias_ref[6 + (m0 - ks) // HALF, 0])
            _, l_new, acc_new = merge(cat(st4[r]), new)
            gate = ga_ref[0, 0, r, base:base + TQ, :].astype(f32)
            o_ref[0, 0, r, base:base + TQ, :] = (acc_new / l_new * gate).astype(bf16)


def _attention(q16, k16, v16, kn, vn, ga16, bias, headmask):
    B, _, _, M, _ = q16.shape
    S = M * RES
    nsub = min(MAX_SUBTILES, M // TQ)
    rows = nsub * TQ
    tile_spec = pl.BlockSpec((1, 1, RES, rows, LANES), lambda j, b, t: (b, j, 0, t, 0))
    seq16_spec = pl.BlockSpec((1, 1, RES, M, LANES), lambda j, b, t: (b, j, 0, 0, 0))
    nat_spec = pl.BlockSpec((1, 1, S, LANES), lambda j, b, t: (b, j, 0, 0))
    return pl.pallas_call(
        functools.partial(_attn_kernel, M=M, S=S, nsub=nsub),
        grid=(N_PAIRS, B, M // rows),
        in_specs=[tile_spec, seq16_spec, seq16_spec, nat_spec, nat_spec, tile_spec,
                  pl.BlockSpec((9, 1, 2 * TQ, TK), lambda j, b, t: (0, j, 0, 0)),
                  pl.BlockSpec((2, LANES), lambda j, b, t: (0, 0))],
        out_specs=tile_spec,
        out_shape=jax.ShapeDtypeStruct((B, N_PAIRS, RES, M, LANES), bf16),
        compiler_params=pltpu.CompilerParams(
            dimension_semantics=("parallel", "parallel", "arbitrary"), vmem_limit_bytes=VMEM_LIMIT),
        name="attn",
    )(q16, k16, v16, kn, vn, ga16, bias, headmask)


def _cmul_const(z, k, n):
    re, im = z
    if k == 0:
        return z
    if 4 * k == n:
        return -im, re
    c = math.cos(2.0 * math.pi * k / n)
    s = math.sin(2.0 * math.pi * k / n)
    return re * c - im * s, im * c + re * s


def _fft_list(xs):
    n = len(xs)
    if n == 1:
        return xs
    even = _fft_list(xs[0::2])
    odd = _fft_list(xs[1::2])
    out = [None] * n
    for k in range(n // 2):
        t = _cmul_const(odd[k], k, n)
        out[k] = (even[k][0] + t[0], even[k][1] + t[1])
        out[k + n // 2] = (even[k][0] - t[0], even[k][1] - t[1])
    return out


def _fnet_kernel(p_ref, q_ref, gf_ref, tw_ref, d_ref, o_ref, br_ref, bi_ref, *, M):
    rb = 16

    def butterfly_body(i, carry):
        row = pl.multiple_of(i * rb, rb)
        tw = tw_ref[pl.ds(row, rb), :]
        for half in range(D_FOURIER // LANES):
            lanes = slice(half * LANES, (half + 1) * LANES)
            zs = []
            for s2 in range(RES):
                rows = pl.ds(pl.multiple_of(s2 * M + row, rb), rb)
                zs.append((p_ref[0, rows, lanes].astype(f32), q_ref[0, rows, lanes].astype(f32)))
            ys = _fft_list(zs)
            for k2 in range(RES):
                re, im = ys[k2]
                if k2:
                    c = tw[:, k2:k2 + 1]
                    s = tw[:, RES + k2:RES + k2 + 1]
                    re, im = re * c - im * s, im * c + re * s
                br_ref[k2, pl.ds(row, rb), lanes] = re.astype(bf16)
                bi_ref[k2, pl.ds(row, rb), lanes] = im.astype(bf16)
        return carry

    lax.fori_loop(0, M // rb, butterfly_body, 0)

    def dft_body(i, carry):
        for u in range(DFT_UNROLL):
            k2 = i * DFT_UNROLL + u
            rhs = jnp.concatenate([br_ref[k2], bi_ref[k2]], axis=0)
            res = jnp.dot(d_ref[...], rhs, preferred_element_type=f32)
            o_ref[0, k2] = (res * gf_ref[0, k2].astype(f32)).astype(bf16)
        return carry

    lax.fori_loop(0, RES // DFT_UNROLL, dft_body, 0)


def _fourier(p, q, gf16, tw, dmat):
    B, S, _ = p.shape
    M = S // RES
    seq_spec = pl.BlockSpec((1, S, D_FOURIER), lambda b: (b, 0, 0))
    res_spec = pl.BlockSpec((1, RES, M, D_FOURIER), lambda b: (b, 0, 0, 0))
    return pl.pallas_call(
        functools.partial(_fnet_kernel, M=M),
        grid=(B,),
        in_specs=[seq_spec, seq_spec, res_spec,
                  pl.BlockSpec((M, LANES), lambda b: (0, 0)),
                  pl.BlockSpec((M, 2 * M), lambda b: (0, 0))],
        out_specs=res_spec,
        out_shape=jax.ShapeDtypeStruct((B, RES, M, D_FOURIER), bf16),
        scratch_shapes=[pltpu.VMEM((RES, M, D_FOURIER), bf16)] * 2,
        compiler_params=pltpu.CompilerParams(
            dimension_semantics=("parallel",), vmem_limit_bytes=VMEM_LIMIT),
        name="fnet",
    )(p, q, gf16, tw, dmat)


def _outproj_kernel(a_ref, f_ref, x_ref, w_ref, y_ref, scr_ref, *, tmm):
    tm = RES * tmm
    attn = jnp.concatenate([a_ref[0, j].reshape(tm, LANES) for j in range(N_PAIRS)], axis=-1)
    mix = jnp.concatenate([attn, f_ref[0].reshape(tm, D_FOURIER)], axis=-1)
    y = jnp.dot(mix, w_ref[...], preferred_element_type=f32)
    for c in range(D_MODEL // LANES):
        for r in range(RES):
            scr_ref[c, pl.ds(r, tmm, stride=PITCH), :] = y[r * tmm:(r + 1) * tmm, c * LANES:(c + 1) * LANES]
    for c in range(D_MODEL // LANES):
        cols = slice(c * LANES, (c + 1) * LANES)
        for g in range(tmm):
            rows = slice(g * RES, (g + 1) * RES)
            y_ref[0, rows, cols] = x_ref[0, rows, cols] + scr_ref[c, g * PITCH:g * PITCH + RES, :]


def _outproj(attn16, four16, x, w_out, tmm=64):
    B, S, _ = x.shape
    M = S // RES
    tm = RES * tmm
    x_spec = pl.BlockSpec((1, tm, D_MODEL), lambda b, i: (b, i, 0))
    return pl.pallas_call(
        functools.partial(_outproj_kernel, tmm=tmm),
        grid=(B, M // tmm),
        in_specs=[pl.BlockSpec((1, N_PAIRS, RES, tmm, LANES), lambda b, i: (b, 0, 0, i, 0)),
                  pl.BlockSpec((1, RES, tmm, D_FOURIER), lambda b, i: (b, 0, i, 0)),
                  x_spec,
                  pl.BlockSpec((D_MODEL, D_MODEL), lambda b, i: (0, 0))],
        out_specs=x_spec,
        out_shape=jax.ShapeDtypeStruct((B, S, D_MODEL), f32),
        scratch_shapes=[pltpu.VMEM((D_MODEL // LANES, tmm * PITCH, LANES), f32)],
        compiler_params=pltpu.CompilerParams(
            dimension_semantics=("parallel", "parallel"), vmem_limit_bytes=VMEM_LIMIT),
        name="outproj",
    )(attn16, four16, x, w_out)


def _encoder_layer(x, norm_g, w_ext, qg, kg, bd, bias, headmask, w_out):
    tw, dmat = _dft_tables(x.shape[1])
    q16, k16, v16, ga16, gf16, kn, vn, p, q = _inproj(x, norm_g, w_ext, qg, kg, bd)
    attn16 = _attention(q16, k16, v16, kn, vn, ga16, bias, headmask)
    four16 = _fourier(p, q, gf16, jnp.asarray(tw), jnp.asarray(dmat, bf16))
    return _outproj(attn16, four16, x, w_out)


def _layer_params(norm_g, w_in, q_norm_g, k_norm_g, rel_bias, w_four, w_out):
    bias = _bias_tiles(rel_bias)
    headmask = jnp.asarray(np.stack([np.arange(LANES) < HEAD_DIM, np.arange(LANES) >= HEAD_DIM]), bf16)
    bd = jnp.asarray(np.kron(np.eye(MXU // HEAD_DIM), np.full((HEAD_DIM, HEAD_DIM), 1.0 / HEAD_DIM)), bf16)
    w_p, w_q = _fold_fourier_weights(w_in[:, 4 * D_ATTN:4 * D_ATTN + D_FOURIER], w_four)
    w_ext = jnp.concatenate(
        [w_in[:, :4 * D_ATTN], w_in[:, 4 * D_ATTN + D_FOURIER:], w_p, w_q], axis=1).astype(bf16)
    qg = jnp.tile(q_norm_g, N_HEADS)[None, :] * (LOG2E / math.sqrt(HEAD_DIM))
    kg = jnp.tile(k_norm_g, N_HEADS)[None, :]
    return (norm_g[None, :], w_ext, qg, kg, bd, bias, headmask, w_out.astype(bf16))


def kernel(x_prompt, x_sample, norm_g, w_in, q_norm_g, k_norm_g, rel_bias, w_four, w_out):
    y_prompt, y_sample = x_prompt, x_sample
    for layer in range(norm_g.shape[0]):
        args = _layer_params(norm_g[layer], w_in[layer], q_norm_g[layer], k_norm_g[layer], rel_bias,
                             w_four[layer], w_out[layer])
        y_prompt = _encoder_layer(y_prompt, *args)
        y_sample = _encoder_layer(y_sample, *args)
    return (y_prompt, y_sample)
```

```python
import functools
import math

import numpy as np
import jax
import jax.numpy as jnp
from jax import lax
from jax.experimental import pallas as pl
from jax.experimental.pallas import tpu as pltpu

D_MODEL = 1024
HEAD_DIM = 64
N_HEADS = 12
N_PAIRS = N_HEADS // 2
D_ATTN = N_HEADS * HEAD_DIM
N_GROUPS = 4
GROUP = 64
D_FOURIER = N_GROUPS * GROUP
N_BUCKETS = 32
MAX_DISTANCE = 1024
RMS_EPS = 1e-6
NEG = -1e30
RES = 16
HALF = 64
TQ = 128
TK = 256
D4_RES = 4
MXU = 256
MAX_SUBTILES = 4
DFT_UNROLL = 8
LOG2E = math.log2(math.e)
LANES = 128
PITCH = 24
VMEM_LIMIT = 52 * 1024 * 1024

C_Q, C_K, C_V, C_GA, C_GF, C_P, C_QQ, C_END = 0, 768, 1536, 2304, 3072, 3328, 3584, 3840

f32 = jnp.float32
bf16 = jnp.bfloat16


def _t5_bucket_np(rel):
    half = N_BUCKETS // 2
    max_exact = half // 2
    ret = np.where(rel > 0, half, 0)
    n = np.abs(rel)
    nf = np.maximum(n, 1).astype(np.float32)
    large = max_exact + (np.log(nf / np.float32(max_exact)) / np.float32(math.log(MAX_DISTANCE / max_exact))
                         * np.float32(half - max_exact)).astype(np.int32)
    large = np.minimum(large, half - 1)
    return ret + np.where(n < max_exact, n, large)


def _bucket_tiles():
    tiles = np.zeros((3, 3, TQ, TK), np.int32)
    i = np.arange(TQ)[:, None]
    c = np.arange(TK)[None, :]
    for var in range(3):
        rel = c - HALF * var - (RES * (i % (TQ // RES)) + i // (TQ // RES))
        tiles[0, var] = np.where(np.abs(rel) <= HALF, _t5_bucket_np(rel * 1), -1)
        qr, kr = TQ // D4_RES, TK // D4_RES
        rel = D4_RES * ((c % kr) - (i % qr) - (HALF // D4_RES) * var) + (c // kr - i // qr)
        tiles[1, var] = np.where(np.abs(rel) <= HALF, _t5_bucket_np(rel * D4_RES), -1)
        rel = c - HALF * var - i
        tiles[2, var] = np.where(np.abs(rel) <= HALF, _t5_bucket_np(rel * RES), -1)
    return tiles.reshape(9, TQ, TK)


def _dft_tables(seq):
    m = seq // RES
    s1 = np.arange(m)[:, None]
    k2 = np.arange(RES)[None, :]
    ang = 2.0 * np.pi * ((s1 * k2) % seq) / seq
    tw = np.zeros((m, LANES), np.float32)
    tw[:, :RES] = np.cos(ang)
    tw[:, RES:2 * RES] = np.sin(ang)
    k1 = np.arange(m)[:, None]
    ang = 2.0 * np.pi * ((k1 * np.arange(m)[None, :]) % m) / m
    scale = 1.0 / math.sqrt(seq)
    dmat = np.concatenate([np.cos(ang) * scale, -np.sin(ang) * scale], axis=1)
    return tw, dmat.astype(np.float32)


def _channel_dft_blockdiag():
    c = np.arange(GROUP)
    ang = 2.0 * np.pi * ((c[:, None] * c[None, :]) % GROUP) / GROUP
    eye = np.eye(N_GROUPS)
    scale = 1.0 / math.sqrt(GROUP)
    return (np.kron(eye, np.cos(ang) * scale).astype(np.float32),
            np.kron(eye, np.sin(ang) * scale).astype(np.float32))


def _bias_kernel(rb_ref, idx_ref, out_ref, *, present):
    for t in range(len(present)):
        idx = idx_ref[t]
        hits = [(b, idx == b) for b in present[t]]
        for head in range(N_HEADS):
            tile = jnp.full((TQ, TK), NEG, f32)
            for b, hit in hits:
                tile = jnp.where(hit, rb_ref[b, head] * LOG2E, tile)
            out_ref[t, head // 2, (head % 2) * TQ:(head % 2 + 1) * TQ, :] = tile


def _bias_tiles(rel_bias):
    idx = _bucket_tiles()
    present = tuple(tuple(int(b) for b in np.unique(tile) if b >= 0) for tile in idx)
    return pl.pallas_call(
        functools.partial(_bias_kernel, present=present),
        in_specs=[pl.BlockSpec(memory_space=pltpu.SMEM), pl.BlockSpec(memory_space=pltpu.VMEM)],
        out_specs=pl.BlockSpec(memory_space=pltpu.VMEM),
        out_shape=jax.ShapeDtypeStruct((9, N_PAIRS, 2 * TQ, TK), f32),
        compiler_params=pltpu.CompilerParams(vmem_limit_bytes=VMEM_LIMIT),
        name="bias_tiles",
    )(rel_bias, jnp.asarray(idx))


def _fold_kernel(wu_ref, ww_ref, cbd_ref, sbd_ref, wp_ref, wq_ref):
    hi = lax.Precision.HIGHEST
    gp = jnp.dot(cbd_ref[...], ww_ref[...], precision=hi, preferred_element_type=f32)
    gq = jnp.dot(sbd_ref[...], ww_ref[...], precision=hi, preferred_element_type=f32)
    wp_ref[...] = jnp.dot(wu_ref[...], gp, precision=hi, preferred_element_type=f32)
    wq_ref[...] = jnp.dot(wu_ref[...], gq, precision=hi, preferred_element_type=f32)


def _fold_fourier_weights(w_u, w_four):
    cbd, sbd = _channel_dft_blockdiag()
    blockmask = jnp.asarray(np.kron(np.eye(N_GROUPS), np.ones((GROUP, GROUP))).astype(np.float32))
    wwide = jnp.tile(w_four.reshape(D_FOURIER, GROUP), (1, N_GROUPS)) * blockmask
    return pl.pallas_call(
        _fold_kernel,
        out_shape=(jax.ShapeDtypeStruct((D_MODEL, D_FOURIER), f32),
                   jax.ShapeDtypeStruct((D_MODEL, D_FOURIER), f32)),
        name="fold_fourier",
    )(w_u, wwide, jnp.asarray(cbd), jnp.asarray(sbd))


def _silu(x):
    return x * (1.0 / (1.0 + jnp.exp(-x)))


def _inproj_kernel(x_ref, g_ref, w_ref, qg_ref, kg_ref, bd_ref,
                   q16_ref, k16_ref, v16_ref, ga16_ref, gf16_ref, kn_ref, vn_ref, p_ref, qq_ref,
                   scr_ref, *, tmm):
    xs = x_ref[0]
    ms = jnp.mean(xs * xs, axis=-1, keepdims=True)
    h = (xs * lax.rsqrt(ms + RMS_EPS) * g_ref[...]).astype(bf16)

    def proj(lo, hi):
        return jnp.dot(h, w_ref[:, lo:hi], preferred_element_type=f32)

    def head_norm(t, gain):
        t2 = (t * t).astype(bf16)
        msq = jnp.concatenate(
            [jnp.dot(t2[:, c * MXU:(c + 1) * MXU], bd_ref[...], preferred_element_type=f32)
             for c in range(D_ATTN // MXU)], axis=-1)
        return t * lax.rsqrt(msq + RMS_EPS) * gain

    def residue_rows(slab, val):
        for g in range(tmm):
            scr_ref[slab, g * PITCH:g * PITCH + RES, :] = val[g * RES:(g + 1) * RES]
        return [scr_ref[slab, pl.ds(r, tmm, stride=PITCH), :] for r in range(RES)]

    def put_res(ref, val, first_slab):
        for j in range(N_PAIRS):
            rows = residue_rows(first_slab + j, val[:, j * LANES:(j + 1) * LANES])
            for r in range(RES):
                ref[0, j, r] = rows[r].astype(bf16)

    def put_nat(ref, val):
        for j in range(N_PAIRS):
            ref[0, j] = val[:, j * LANES:(j + 1) * LANES].astype(bf16)

    q_raw = proj(C_Q, C_K)
    k_raw = proj(C_K, C_V)
    qn = head_norm(q_raw, qg_ref[...])
    put_res(q16_ref, qn, 0)
    v = proj(C_V, C_GA)
    kn = head_norm(k_raw, kg_ref[...])
    put_res(k16_ref, kn, 6)
    put_nat(kn_ref, kn)
    put_res(v16_ref, v, 12)
    put_nat(vn_ref, v)
    put_res(ga16_ref, _silu(proj(C_GA, C_GF)), 18)
    half = h.shape[0] // 2
    four = jnp.concatenate(
        [jnp.dot(h[:half], w_ref[:, C_GF:C_END], preferred_element_type=f32),
         jnp.dot(h[half:], w_ref[:, C_GF:C_END], preferred_element_type=f32)], axis=0)
    gf = _silu(four[:, :D_FOURIER])
    for c in range(D_FOURIER // LANES):
        rows = residue_rows(24 + c, gf[:, c * LANES:(c + 1) * LANES])
        for r in range(RES):
            gf16_ref[0, r, :, c * LANES:(c + 1) * LANES] = rows[r].astype(bf16)
    p_ref[0] = four[:, C_P - C_GF:C_QQ - C_GF].astype(bf16)
    qq_ref[0] = four[:, C_QQ - C_GF:].astype(bf16)


def _inproj(x, norm_g, w_ext, qg, kg, bd, tmm=32):
    B, S, _ = x.shape
    M = S // RES
    tm = RES * tmm
    res_shape = jax.ShapeDtypeStruct((B, N_PAIRS, RES, M, LANES), bf16)
    nat_shape = jax.ShapeDtypeStruct((B, N_PAIRS, S, LANES), bf16)
    four_shape = jax.ShapeDtypeStruct((B, S, D_FOURIER), bf16)
    res_spec = pl.BlockSpec((1, N_PAIRS, RES, tmm, LANES), lambda b, i: (b, 0, 0, i, 0))
    nat_spec = pl.BlockSpec((1, N_PAIRS, tm, LANES), lambda b, i: (b, 0, i, 0))
    four_spec = pl.BlockSpec((1, tm, D_FOURIER), lambda b, i: (b, i, 0))
    const = lambda shape: pl.BlockSpec(shape, lambda b, i: (0,) * len(shape))
    n_slabs = (4 * D_ATTN + D_FOURIER) // LANES
    return pl.pallas_call(
        functools.partial(_inproj_kernel, tmm=tmm),
        grid=(B, M // tmm),
        in_specs=[pl.BlockSpec((1, tm, D_MODEL), lambda b, i: (b, i, 0)),
                  const((1, D_MODEL)), const((D_MODEL, C_END)),
                  const((1, D_ATTN)), const((1, D_ATTN)), const((MXU, MXU))],
        out_specs=[res_spec, res_spec, res_spec, res_spec,
                   pl.BlockSpec((1, RES, tmm, D_FOURIER), lambda b, i: (b, 0, i, 0)),
                   nat_spec, nat_spec, four_spec, four_spec],
        out_shape=[res_shape, res_shape, res_shape, res_shape,
                   jax.ShapeDtypeStruct((B, RES, M, D_FOURIER), bf16),
                   nat_shape, nat_shape, four_shape, four_shape],
        scratch_shapes=[pltpu.VMEM((n_slabs, tmm * PITCH, LANES), f32)],
        compiler_params=pltpu.CompilerParams(
            dimension_semantics=("parallel", "parallel"), vmem_limit_bytes=VMEM_LIMIT),
        name="inproj",
    )(x, norm_g, w_ext, qg, kg, bd)


def _attn_kernel(q_ref, k16_ref, v16_ref, kn_ref, vn_ref, ga_ref, bias_ref, hm_ref, o_ref, *, M, S, nsub):
    is_a = lax.broadcasted_iota(jnp.int32, (TQ, LANES), 1) < HEAD_DIM
    mask_a = hm_ref[0:1, :]
    mask_b = hm_ref[1:2, :]
    ones = jnp.ones((TK, LANES), bf16)

    def tile(q, k, v, bias):
        qs = jnp.concatenate([q * mask_a, q * mask_b], axis=0)
        s = lax.dot_general(qs, k, (((1,), (1,)), ((), ())), preferred_element_type=f32) + bias
        mx = jnp.max(s, axis=-1, keepdims=True)
        p = jnp.exp2((s - mx).astype(bf16))
        pv = jnp.dot(p, jnp.concatenate([v, ones], axis=1), preferred_element_type=f32)
        o = jnp.where(is_a, pv[:TQ, :LANES], pv[TQ:, :LANES])
        lrep = jnp.where(is_a, pv[:TQ, LANES:], pv[TQ:, LANES:])
        mrep = jnp.where(is_a, mx[:TQ], mx[TQ:])
        return mrep, lrep, o

    def merge(old, new):
        (m_old, l_old, acc_old), (mr, lr, o) = old, new
        m_new = jnp.maximum(m_old, mr)
        alpha = jnp.exp2(m_old - m_new)
        beta = jnp.exp2(mr - m_new)
        return m_new, l_old * alpha + lr * beta, acc_old * alpha + o * beta

    def rows(stat, lo, n):
        return tuple(x[lo:lo + n] for x in stat)

    def cat(stats):
        return tuple(jnp.concatenate(xs, axis=0) for xs in zip(*stats))

    for sub in range(nsub):
        base = sub * TQ
        m0 = (pl.program_id(2) * nsub + sub) * TQ

        st1 = [[None] * (TQ // 8) for _ in range(RES)]
        for jj2 in range(TQ // 16):
            qf = q_ref[0, 0, :, base + 16 * jj2:base + 16 * jj2 + 16, :].astype(f32)
            for par in range(2):
                jj = 2 * jj2 + par
                q = qf[:, 8 * par:8 * par + 8, :].reshape(TQ, LANES).astype(bf16)
                t0 = RES * (m0 + 8 * jj)
                ks = pl.multiple_of(jnp.clip(t0 - HALF, 0, S - TK), HALF)
                stat = tile(q, kn_ref[0, 0, pl.ds(ks, TK), :], vn_ref[0, 0, pl.ds(ks, TK), :],
                            bias_ref[(t0 - ks) // HALF, 0])
                for r in range(RES):
                    st1[r][jj] = rows(stat, 8 * r, 8)

        qr, kr, side = TQ // D4_RES, TK // D4_RES, HALF // D4_RES
        st4 = [[None] * D4_RES for _ in range(RES)]
        for b in range(D4_RES):
            for j4 in range(TQ // qr):
                ml = base + qr * j4
                mg = m0 + qr * j4
                ks = pl.multiple_of(jnp.clip(mg - side, 0, M - kr), side)
                res4 = [D4_RES * a + b for a in range(RES // D4_RES)]
                q = jnp.concatenate([q_ref[0, 0, r, ml:ml + qr, :] for r in res4], axis=0)
                k = jnp.concatenate([k16_ref[0, 0, r, pl.ds(ks, kr), :] for r in res4], axis=0)
                v = jnp.concatenate([v16_ref[0, 0, r, pl.ds(ks, kr), :] for r in res4], axis=0)
                old = cat([st1[r][(qr // 8) * j4 + c] for r in res4 for c in range(qr // 8)])
                stat = merge(old, tile(q, k, v, bias_ref[3 + (mg - ks) // side, 0]))
                for a, r in enumerate(res4):
                    st4[r][j4] = rows(stat, qr * a, qr)

        ks = pl.multiple_of(jnp.clip(m0 - HALF, 0, M - TK), HALF)
        for r in range(RES):
            new = tile(q_ref[0, 0, r, base:base + TQ, :], k16_ref[0, 0, r, pl.ds(ks, TK), :],
                       v16_ref[0, 0, r, pl.ds(ks, TK), :], bias_ref[6 + (m0 - ks) // HALF, 0])
            _, l_new, acc_new = merge(cat(st4[r]), new)
            gate = ga_ref[0, 0, r, base:base + TQ, :].astype(f32)
            o_ref[0, 0, r, base:base + TQ, :] = (acc_new / l_new * gate).astype(bf16)


def _attention(q16, k16, v16, kn, vn, ga16, bias, headmask):
    B, _, _, M, _ = q16.shape
    S = M * RES
    nsub = min(MAX_SUBTILES, M // TQ)
    rows = nsub * TQ
    tile_spec = pl.BlockSpec((1, 1, RES, rows, LANES), lambda j, b, t: (b, j, 0, t, 0))
    seq16_spec = pl.BlockSpec((1, 1, RES, M, LANES), lambda j, b, t: (b, j, 0, 0, 0))
    nat_spec = pl.BlockSpec((1, 1, S, LANES), lambda j, b, t: (b, j, 0, 0))
    return pl.pallas_call(
        functools.partial(_attn_kernel, M=M, S=S, nsub=nsub),
        grid=(N_PAIRS, B, M // rows),
        in_specs=[tile_spec, seq16_spec, seq16_spec, nat_spec, nat_spec, tile_spec,
                  pl.BlockSpec((9, 1, 2 * TQ, TK), lambda j, b, t: (0, j, 0, 0)),
                  pl.BlockSpec((2, LANES), lambda j, b, t: (0, 0))],
        out_specs=tile_spec,
        out_shape=jax.ShapeDtypeStruct((B, N_PAIRS, RES, M, LANES), bf16),
        compiler_params=pltpu.CompilerParams(
            dimension_semantics=("parallel", "parallel", "arbitrary"), vmem_limit_bytes=VMEM_LIMIT),
        name="attn",
    )(q16, k16, v16, kn, vn, ga16, bias, headmask)


def _cmul_const(z, k, n):
    re, im = z
    if k == 0:
        return z
    if 4 * k == n:
        return -im, re
    c = math.cos(2.0 * math.pi * k / n)
    s = math.sin(2.0 * math.pi * k / n)
    return re * c - im * s, im * c + re * s


def _fft_list(xs):
    n = len(xs)
    if n == 1:
        return xs
    even = _fft_list(xs[0::2])
    odd = _fft_list(xs[1::2])
    out = [None] * n
    for k in range(n // 2):
        t = _cmul_const(odd[k], k, n)
        out[k] = (even[k][0] + t[0], even[k][1] + t[1])
        out[k + n // 2] = (even[k][0] - t[0], even[k][1] - t[1])
    return out


def _fnet_kernel(p_ref, q_ref, gf_ref, tw_ref, d_ref, o_ref, br_ref, bi_ref, *, M):
    rb = 16

    def butterfly_body(i, carry):
        row = pl.multiple_of(i * rb, rb)
        tw = tw_ref[pl.ds(row, rb), :]
        for half in range(D_FOURIER // LANES):
            lanes = slice(half * LANES, (half + 1) * LANES)
            zs = []
            for s2 in range(RES):
                rows = pl.ds(pl.multiple_of(s2 * M + row, rb), rb)
                zs.append((p_ref[0, rows, lanes].astype(f32), q_ref[0, rows, lanes].astype(f32)))
            ys = _fft_list(zs)
            for k2 in range(RES):
                re, im = ys[k2]
                if k2:
                    c = tw[:, k2:k2 + 1]
                    s = tw[:, RES + k2:RES + k2 + 1]
                    re, im = re * c - im * s, im * c + re * s
                br_ref[k2, pl.ds(row, rb), lanes] = re.astype(bf16)
                bi_ref[k2, pl.ds(row, rb), lanes] = im.astype(bf16)
        return carry

    lax.fori_loop(0, M // rb, butterfly_body, 0)

    def dft_body(i, carry):
        for u in range(DFT_UNROLL):
            k2 = i * DFT_UNROLL + u
            rhs = jnp.concatenate([br_ref[k2], bi_ref[k2]], axis=0)
            res = jnp.dot(d_ref[...], rhs, preferred_element_type=f32)
            o_ref[0, k2] = (res * gf_ref[0, k2].astype(f32)).astype(bf16)
        return carry

    lax.fori_loop(0, RES // DFT_UNROLL, dft_body, 0)


def _fourier(p, q, gf16, tw, dmat):
    B, S, _ = p.shape
    M = S // RES
    seq_spec = pl.BlockSpec((1, S, D_FOURIER), lambda b: (b, 0, 0))
    res_spec = pl.BlockSpec((1, RES, M, D_FOURIER), lambda b: (b, 0, 0, 0))
    return pl.pallas_call(
        functools.partial(_fnet_kernel, M=M),
        grid=(B,),
        in_specs=[seq_spec, seq_spec, res_spec,
                  pl.BlockSpec((M, LANES), lambda b: (0, 0)),
                  pl.BlockSpec((M, 2 * M), lambda b: (0, 0))],
        out_specs=res_spec,
        out_shape=jax.ShapeDtypeStruct((B, RES, M, D_FOURIER), bf16),
        scratch_shapes=[pltpu.VMEM((RES, M, D_FOURIER), bf16)] * 2,
        compiler_params=pltpu.CompilerParams(
            dimension_semantics=("parallel",), vmem_limit_bytes=VMEM_LIMIT),
        name="fnet",
    )(p, q, gf16, tw, dmat)


def _outproj_kernel(a_ref, f_ref, x_ref, w_ref, y_ref, scr_ref, *, tmm):
    tm = RES * tmm
    attn = jnp.concatenate([a_ref[0, j].reshape(tm, LANES) for j in range(N_PAIRS)], axis=-1)
    mix = jnp.concatenate([attn, f_ref[0].reshape(tm, D_FOURIER)], axis=-1)
    y = jnp.dot(mix, w_ref[...], preferred_element_type=f32)
    for c in range(D_MODEL // LANES):
        for r in range(RES):
            scr_ref[c, pl.ds(r, tmm, stride=PITCH), :] = y[r * tmm:(r + 1) * tmm, c * LANES:(c + 1) * LANES]
    for c in range(D_MODEL // LANES):
        cols = slice(c * LANES, (c + 1) * LANES)
        for g in range(tmm):
            rows = slice(g * RES, (g + 1) * RES)
            y_ref[0, rows, cols] = x_ref[0, rows, cols] + scr_ref[c, g * PITCH:g * PITCH + RES, :]


def _outproj(attn16, four16, x, w_out, tmm=64):
    B, S, _ = x.shape
    M = S // RES
    tm = RES * tmm
    x_spec = pl.BlockSpec((1, tm, D_MODEL), lambda b, i: (b, i, 0))
    return pl.pallas_call(
        functools.partial(_outproj_kernel, tmm=tmm),
        grid=(B, M // tmm),
        in_specs=[pl.BlockSpec((1, N_PAIRS, RES, tmm, LANES), lambda b, i: (b, 0, 0, i, 0)),
                  pl.BlockSpec((1, RES, tmm, D_FOURIER), lambda b, i: (b, 0, i, 0)),
                  x_spec,
                  pl.BlockSpec((D_MODEL, D_MODEL), lambda b, i: (0, 0))],
        out_specs=x_spec,
        out_shape=jax.ShapeDtypeStruct((B, S, D_MODEL), f32),
        scratch_shapes=[pltpu.VMEM((D_MODEL // LANES, tmm * PITCH, LANES), f32)],
        compiler_params=pltpu.CompilerParams(
            dimension_semantics=("parallel", "parallel"), vmem_limit_bytes=VMEM_LIMIT),
        name="outproj",
    )(attn16, four16, x, w_out)


def _encoder_layer(x, norm_g, w_ext, qg, kg, bd, bias, headmask, w_out):
    tw, dmat = _dft_tables(x.shape[1])
    q16, k16, v16, ga16, gf16, kn, vn, p, q = _inproj(x, norm_g, w_ext, qg, kg, bd)
    attn16 = _attention(q16, k16, v16, kn, vn, ga16, bias, headmask)
    four16 = _fourier(p, q, gf16, jnp.asarray(tw), jnp.asarray(dmat, bf16))
    return _outproj(attn16, four16, x, w_out)


def _layer_params(norm_g, w_in, q_norm_g, k_norm_g, rel_bias, w_four, w_out):
    bias = _bias_tiles(rel_bias)
    headmask = jnp.asarray(np.stack([np.arange(LANES) < HEAD_DIM, np.arange(LANES) >= HEAD_DIM]), bf16)
    bd = jnp.asarray(np.kron(np.eye(MXU // HEAD_DIM), np.full((HEAD_DIM, HEAD_DIM), 1.0 / HEAD_DIM)), bf16)
    w_p, w_q = _fold_fourier_weights(w_in[:, 4 * D_ATTN:4 * D_ATTN + D_FOURIER], w_four)
    w_ext = jnp.concatenate(
        [w_in[:, :4 * D_ATTN], w_in[:, 4 * D_ATTN + D_FOURIER:], w_p, w_q], axis=1).astype(bf16)
    qg = jnp.tile(q_norm_g, N_HEADS)[None, :] * (LOG2E / math.sqrt(HEAD_DIM))
    kg = jnp.tile(k_norm_g, N_HEADS)[None, :]
    return (norm_g[None, :], w_ext, qg, kg, bd, bias, headmask, w_out.astype(bf16))


def kernel(x_prompt, x_sample, norm_g, w_in, q_norm_g, k_norm_g, rel_bias, w_four, w_out):
    y_prompt, y_sample = x_prompt, x_sample
    for layer in range(norm_g.shape[0]):
        args = _layer_params(norm_g[layer], w_in[layer], q_norm_g[layer], k_norm_g[layer], rel_bias,
                             w_four[layer], w_out[layer])
        y_prompt = _encoder_layer(y_prompt, *args)
        y_sample = _encoder_layer(y_sample, *args)
    return (y_prompt, y_sample)
```

```python
import functools
import math

import numpy as np
import jax
import jax.numpy as jnp
from jax import lax
from jax.experimental import pallas as pl
from jax.experimental.pallas import tpu as pltpu

D_MODEL = 1024
HEAD_DIM = 64
N_HEADS = 12
N_PAIRS = N_HEADS // 2
D_ATTN = N_HEADS * HEAD_DIM
N_GROUPS = 4
GROUP = 64
D_FOURIER = N_GROUPS * GROUP
N_BUCKETS = 32
MAX_DISTANCE = 1024
RMS_EPS = 1e-6
NEG = -1e30
RES = 16
HALF = 64
TQ = 128
TK = 256
D4_RES = 4
MXU = 256
MAX_SUBTILES = 4
DFT_UNROLL = 8
LOG2E = math.log2(math.e)
LANES = 128
PITCH = 24
VMEM_LIMIT = 52 * 1024 * 1024

C_Q, C_K, C_V, C_GA, C_U, C_GF, C_END = 0, 768, 1536, 2304, 3072, 3328, 3584

f32 = jnp.float32
bf16 = jnp.bfloat16


def _t5_bucket_np(rel):
    half = N_BUCKETS // 2
    max_exact = half // 2
    ret = np.where(rel > 0, half, 0)
    n = np.abs(rel)
    nf = np.maximum(n, 1).astype(np.float32)
    large = max_exact + (np.log(nf / np.float32(max_exact)) / np.float32(math.log(MAX_DISTANCE / max_exact))
                         * np.float32(half - max_exact)).astype(np.int32)
    large = np.minimum(large, half - 1)
    return ret + np.where(n < max_exact, n, large)


def _bucket_tiles():
    tiles = np.zeros((3, 3, TQ, TK), np.int32)
    i = np.arange(TQ)[:, None]
    c = np.arange(TK)[None, :]
    for var in range(3):
        rel = c - HALF * var - (RES * (i % (TQ // RES)) + i // (TQ // RES))
        tiles[0, var] = np.where(np.abs(rel) <= HALF, _t5_bucket_np(rel * 1), -1)
        qr, kr = TQ // D4_RES, TK // D4_RES
        rel = D4_RES * ((c % kr) - (i % qr) - (HALF // D4_RES) * var) + (c // kr - i // qr)
        tiles[1, var] = np.where(np.abs(rel) <= HALF, _t5_bucket_np(rel * D4_RES), -1)
        rel = c - HALF * var - i
        tiles[2, var] = np.where(np.abs(rel) <= HALF, _t5_bucket_np(rel * RES), -1)
    return tiles.reshape(9, TQ, TK)


def _dft_tables(seq):
    m = seq // RES
    s1 = np.arange(m)[:, None]
    k2 = np.arange(RES)[None, :]
    ang = 2.0 * np.pi * ((s1 * k2) % seq) / seq
    tw = np.zeros((m, LANES), np.float32)
    tw[:, :RES] = np.cos(ang)
    tw[:, RES:2 * RES] = np.sin(ang)
    k1 = np.arange(m)[:, None]
    ang = 2.0 * np.pi * ((k1 * np.arange(m)[None, :]) % m) / m
    scale = 1.0 / math.sqrt(seq)
    dmat = np.concatenate([np.cos(ang) * scale, -np.sin(ang) * scale], axis=1)
    return tw, dmat.astype(np.float32)


def _channel_dft_blockdiag():
    c = np.arange(GROUP)
    ang = 2.0 * np.pi * ((c[:, None] * c[None, :]) % GROUP) / GROUP
    eye = np.eye(N_GROUPS)
    scale = 1.0 / math.sqrt(GROUP)
    return (np.kron(eye, np.cos(ang) * scale).astype(np.float32),
            np.kron(eye, np.sin(ang) * scale).astype(np.float32))


def _bias_kernel(rbt_ref, idx_ref, out_ref):
    lane = lax.broadcasted_iota(jnp.int32, (1, LANES), 1)
    for head in range(N_HEADS):
        row = jnp.where(lane == N_BUCKETS, NEG, rbt_ref[head:head + 1, :] * LOG2E)
        table = jnp.broadcast_to(row, (TQ, LANES))
        for t in range(idx_ref.shape[0]):
            for c in range(TK // LANES):
                cols = slice(c * LANES, (c + 1) * LANES)
                out_ref[t, head // 2, (head % 2) * TQ:(head % 2 + 1) * TQ, cols] = jnp.take_along_axis(
                    table, idx_ref[t, :, cols], axis=1)


def _bias_tiles(rel_bias):
    idx = _bucket_tiles()
    idx = np.where(idx < 0, N_BUCKETS, idx).astype(np.int32)
    rbt = jnp.pad(rel_bias.T, ((0, 0), (0, LANES - N_BUCKETS)))
    return pl.pallas_call(
        _bias_kernel,
        out_shape=jax.ShapeDtypeStruct((9, N_PAIRS, 2 * TQ, TK), f32),
        compiler_params=pltpu.CompilerParams(vmem_limit_bytes=VMEM_LIMIT),
        name="bias_tiles",
    )(rbt, jnp.asarray(idx))


def _weights_kernel(win_ref, wout_ref, ww_ref, cbd_ref, sbd_ref, winb_ref, woutb_ref, g_ref):
    hi = lax.Precision.HIGHEST
    winb_ref[...] = win_ref[...].astype(bf16)
    woutb_ref[...] = wout_ref[...].astype(bf16)
    g_ref[:, :D_FOURIER] = jnp.dot(cbd_ref[...], ww_ref[...], precision=hi, preferred_element_type=f32).astype(bf16)
    g_ref[:, D_FOURIER:] = jnp.dot(sbd_ref[...], ww_ref[...], precision=hi, preferred_element_type=f32).astype(bf16)


def _prepare_weights(w_in, w_four, w_out, row_block=256):
    cbd, sbd = _channel_dft_blockdiag()
    blockmask = jnp.asarray(np.kron(np.eye(N_GROUPS), np.ones((GROUP, GROUP))).astype(np.float32))
    wwide = jnp.tile(w_four.reshape(D_FOURIER, GROUP), (1, N_GROUPS)) * blockmask
    rows = lambda width: pl.BlockSpec((row_block, width), lambda i: (i, 0))
    small = pl.BlockSpec((D_FOURIER, D_FOURIER), lambda i: (0, 0))
    return pl.pallas_call(
        _weights_kernel,
        grid=(D_MODEL // row_block,),
        in_specs=[rows(C_END), rows(D_MODEL), small, small, small],
        out_specs=[rows(C_END), rows(D_MODEL), pl.BlockSpec((D_FOURIER, 2 * D_FOURIER), lambda i: (0, 0))],
        out_shape=(jax.ShapeDtypeStruct((D_MODEL, C_END), bf16),
                   jax.ShapeDtypeStruct((D_MODEL, D_MODEL), bf16),
                   jax.ShapeDtypeStruct((D_FOURIER, 2 * D_FOURIER), bf16)),
        compiler_params=pltpu.CompilerParams(dimension_semantics=("arbitrary",)),
        name="prepare_weights",
    )(w_in, w_out, wwide, jnp.asarray(cbd), jnp.asarray(sbd))


def _silu(x):
    return x * (1.0 / (1.0 + jnp.exp(-x)))


def _inproj_kernel(x_ref, g_ref, w_ref, qg_ref, kg_ref, bd_ref,
                   q16_ref, k16_ref, v16_ref, ga16_ref, gf16_ref, kn_ref, vn_ref, u_ref,
                   scr_ref, *, tmm):
    xs = x_ref[0]
    ms = jnp.mean(xs * xs, axis=-1, keepdims=True)
    h = (xs * lax.rsqrt(ms + RMS_EPS) * g_ref[...]).astype(bf16)

    def proj(lo, hi):
        return jnp.dot(h, w_ref[:, lo:hi], preferred_element_type=f32)

    def head_norm(t, gain):
        t2 = (t * t).astype(bf16)
        msq = jnp.concatenate(
            [jnp.dot(t2[:, c * MXU:(c + 1) * MXU], bd_ref[...], preferred_element_type=f32)
             for c in range(D_ATTN // MXU)], axis=-1)
        return t * lax.rsqrt(msq + RMS_EPS) * gain

    def residue_rows(slab, val):
        for g in range(tmm):
            scr_ref[slab, g * PITCH:g * PITCH + RES, :] = val[g * RES:(g + 1) * RES]
        return [scr_ref[slab, pl.ds(r, tmm, stride=PITCH), :] for r in range(RES)]

    def put_res(ref, val, first_slab):
        for j in range(N_PAIRS):
            rows = residue_rows(first_slab + j, val[:, j * LANES:(j + 1) * LANES])
            for r in range(RES):
                ref[0, j, r] = rows[r].astype(bf16)

    def put_nat(ref, val):
        for j in range(N_PAIRS):
            ref[0, j] = val[:, j * LANES:(j + 1) * LANES].astype(bf16)

    qn = head_norm(proj(C_Q, C_K), qg_ref[...])
    put_res(q16_ref, qn, 0)
    kn = head_norm(proj(C_K, C_V), kg_ref[...])
    put_res(k16_ref, kn, 6)
    put_nat(kn_ref, kn)
    v = proj(C_V, C_GA)
    put_res(v16_ref, v, 12)
    put_nat(vn_ref, v)
    put_res(ga16_ref, _silu(proj(C_GA, C_U)), 18)
    half = h.shape[0] // 2
    four = jnp.concatenate(
        [jnp.dot(h[:half], w_ref[:, C_U:C_END], preferred_element_type=f32),
         jnp.dot(h[half:], w_ref[:, C_U:C_END], preferred_element_type=f32)], axis=0)
    u_ref[0] = four[:, :D_FOURIER].astype(bf16)
    gf = _silu(four[:, D_FOURIER:])
    for c in range(D_FOURIER // LANES):
        rows = residue_rows(24 + c, gf[:, c * LANES:(c + 1) * LANES])
        for r in range(RES):
            gf16_ref[0, r, :, c * LANES:(c + 1) * LANES] = rows[r].astype(bf16)


def _inproj(x, norm_g, w_in, qg, kg, bd, tmm=32):
    B, S, _ = x.shape
    M = S // RES
    tm = RES * tmm
    res_shape = jax.ShapeDtypeStruct((B, N_PAIRS, RES, M, LANES), bf16)
    nat_shape = jax.ShapeDtypeStruct((B, N_PAIRS, S, LANES), bf16)
    four_shape = jax.ShapeDtypeStruct((B, S, D_FOURIER), bf16)
    res_spec = pl.BlockSpec((1, N_PAIRS, RES, tmm, LANES), lambda b, i: (b, 0, 0, i, 0))
    nat_spec = pl.BlockSpec((1, N_PAIRS, tm, LANES), lambda b, i: (b, 0, i, 0))
    four_spec = pl.BlockSpec((1, tm, D_FOURIER), lambda b, i: (b, i, 0))
    const = lambda shape: pl.BlockSpec(shape, lambda b, i: (0,) * len(shape))
    n_slabs = (4 * D_ATTN + D_FOURIER) // LANES
    return pl.pallas_call(
        functools.partial(_inproj_kernel, tmm=tmm),
        grid=(B, M // tmm),
        in_specs=[pl.BlockSpec((1, tm, D_MODEL), lambda b, i: (b, i, 0)),
                  const((1, D_MODEL)), const((D_MODEL, C_END)),
                  const((1, D_ATTN)), const((1, D_ATTN)), const((MXU, MXU))],
        out_specs=[res_spec, res_spec, res_spec, res_spec,
                   pl.BlockSpec((1, RES, tmm, D_FOURIER), lambda b, i: (b, 0, i, 0)),
                   nat_spec, nat_spec, four_spec],
        out_shape=[res_shape, res_shape, res_shape, res_shape,
                   jax.ShapeDtypeStruct((B, RES, M, D_FOURIER), bf16),
                   nat_shape, nat_shape, four_shape],
        scratch_shapes=[pltpu.VMEM((n_slabs, tmm * PITCH, LANES), f32)],
        compiler_params=pltpu.CompilerParams(
            dimension_semantics=("parallel", "parallel"), vmem_limit_bytes=VMEM_LIMIT),
        name="inproj",
    )(x, norm_g, w_in, qg, kg, bd)


def _attn_kernel(q_ref, k16_ref, v16_ref, kn_ref, vn_ref, ga_ref, bias_ref, hm_ref, o_ref, *, M, S, nsub, nbatch):
    is_a = lax.broadcasted_iota(jnp.int32, (TQ, LANES), 1) < HEAD_DIM
    mask_a = hm_ref[0:1, :]
    mask_b = hm_ref[1:2, :]
    ones = jnp.ones((TK, LANES), bf16)

    def tile(q, k, v, bias):
        qs = jnp.concatenate([q * mask_a, q * mask_b], axis=0)
        s = lax.dot_general(qs, k, (((1,), (1,)), ((), ())), preferred_element_type=f32) + bias
        mx = jnp.max(s, axis=-1, keepdims=True)
        p = jnp.exp2((s - mx).astype(bf16))
        pv = jnp.dot(p, jnp.concatenate([v, ones], axis=1), preferred_element_type=f32)
        o = jnp.where(is_a, pv[:TQ, :LANES], pv[TQ:, :LANES])
        lrep = jnp.where(is_a, pv[:TQ, LANES:], pv[TQ:, LANES:])
        mrep = jnp.where(is_a, mx[:TQ], mx[TQ:])
        return mrep, lrep, o

    def merge(old, new):
        (m_old, l_old, acc_old), (mr, lr, o) = old, new
        m_new = jnp.maximum(m_old, mr)
        alpha = jnp.exp2(m_old - m_new)
        beta = jnp.exp2(mr - m_new)
        return m_new, l_old * alpha + lr * beta, acc_old * alpha + o * beta

    def rows(stat, lo, n):
        return tuple(x[lo:lo + n] for x in stat)

    def cat(stats):
        return tuple(jnp.concatenate(xs, axis=0) for xs in zip(*stats))

    for bi, sub in [(bi, sub) for bi in range(nbatch) for sub in range(nsub)]:
        base = sub * TQ
        m0 = (pl.program_id(2) * nsub + sub) * TQ

        st1 = [[None] * (TQ // 8) for _ in range(RES)]
        for jj2 in range(TQ // 16):
            qf = q_ref[bi, 0, :, base + 16 * jj2:base + 16 * jj2 + 16, :].astype(f32)
            for par in range(2):
                jj = 2 * jj2 + par
                q = qf[:, 8 * par:8 * par + 8, :].reshape(TQ, LANES).astype(bf16)
                t0 = RES * (m0 + 8 * jj)
                ks = pl.multiple_of(jnp.clip(t0 - HALF, 0, S - TK), HALF)
                stat = tile(q, kn_ref[bi, 0, pl.ds(ks, TK), :], vn_ref[bi, 0, pl.ds(ks, TK), :],
                            bias_ref[(t0 - ks) // HALF, 0])
                for r in range(RES):
                    st1[r][jj] = rows(stat, 8 * r, 8)

        qr, kr, side = TQ // D4_RES, TK // D4_RES, HALF // D4_RES
        st4 = [[None] * D4_RES for _ in range(RES)]
        for b in range(D4_RES):
            for j4 in range(TQ // qr):
                ml = base + qr * j4
                mg = m0 + qr * j4
                ks = pl.multiple_of(jnp.clip(mg - side, 0, M - kr), side)
                res4 = [D4_RES * a + b for a in range(RES // D4_RES)]
                q = jnp.concatenate([q_ref[bi, 0, r, ml:ml + qr, :] for r in res4], axis=0)
                k = jnp.concatenate([k16_ref[bi, 0, r, pl.ds(ks, kr), :] for r in res4], axis=0)
                v = jnp.concatenate([v16_ref[bi, 0, r, pl.ds(ks, kr), :] for r in res4], axis=0)
                old = cat([st1[r][(qr // 8) * j4 + c] for r in res4 for c in range(qr // 8)])
                stat = merge(old, tile(q, k, v, bias_ref[3 + (mg - ks) // side, 0]))
                for a, r in enumerate(res4):
                    st4[r][j4] = rows(stat, qr * a, qr)

        ks = pl.multiple_of(jnp.clip(m0 - HALF, 0, M - TK), HALF)
        for r in range(RES):
            new = tile(q_ref[bi, 0, r, base:base + TQ, :], k16_ref[bi, 0, r, pl.ds(ks, TK), :],
                       v16_ref[bi, 0, r, pl.ds(ks, TK), :], bias_ref[6 + (m0 - ks) // HALF, 0])
            _, l_new, acc_new = merge(cat(st4[r]), new)
            gate = ga_ref[bi, 0, r, base:base + TQ, :].astype(f32)
            o_ref[bi, 0, r, base:base + TQ, :] = (acc_new / l_new * gate).astype(bf16)


def _attention(q16, k16, v16, kn, vn, ga16, bias, headmask):
    B, _, _, M, _ = q16.shape
    S = M * RES
    nsub = min(MAX_SUBTILES, M // TQ)
    rows = nsub * TQ
    nbatch = math.gcd(B, MAX_SUBTILES // nsub)
    tile_spec = pl.BlockSpec((nbatch, 1, RES, rows, LANES), lambda j, b, t: (b, j, 0, t, 0))
    seq16_spec = pl.BlockSpec((nbatch, 1, RES, M, LANES), lambda j, b, t: (b, j, 0, 0, 0))
    nat_spec = pl.BlockSpec((nbatch, 1, S, LANES), lambda j, b, t: (b, j, 0, 0))
    return pl.pallas_call(
        functools.partial(_attn_kernel, M=M, S=S, nsub=nsub, nbatch=nbatch),
        grid=(N_PAIRS, B // nbatch, M // rows),
        in_specs=[tile_spec, seq16_spec, seq16_spec, nat_spec, nat_spec, tile_spec,
                  pl.BlockSpec((9, 1, 2 * TQ, TK), lambda j, b, t: (0, j, 0, 0)),
                  pl.BlockSpec((2, LANES), lambda j, b, t: (0, 0))],
        out_specs=tile_spec,
        out_shape=jax.ShapeDtypeStruct((B, N_PAIRS, RES, M, LANES), bf16),
        compiler_params=pltpu.CompilerParams(
            dimension_semantics=("parallel", "parallel", "arbitrary"), vmem_limit_bytes=VMEM_LIMIT),
        name="attn",
    )(q16, k16, v16, kn, vn, ga16, bias, headmask)


def _cmul_const(z, k, n):
    re, im = z
    if k == 0:
        return z
    if 4 * k == n:
        return -im, re
    c = math.cos(2.0 * math.pi * k / n)
    s = math.sin(2.0 * math.pi * k / n)
    return re * c - im * s, im * c + re * s


def _fft_list(xs):
    n = len(xs)
    if n == 1:
        return xs
    even = _fft_list(xs[0::2])
    odd = _fft_list(xs[1::2])
    out = [None] * n
    for k in range(n // 2):
        t = _cmul_const(odd[k], k, n)
        out[k] = (even[k][0] + t[0], even[k][1] + t[1])
        out[k + n // 2] = (even[k][0] - t[0], even[k][1] - t[1])
    return out


def _fnet_kernel(u_ref, g_ref, gf_ref, tw_ref, d_ref, o_ref, pq_ref, br_ref, bi_ref, *, M):
    def channel_body(i, carry):
        for c in range(DFT_UNROLL):
            rows = pl.ds(pl.multiple_of((i * DFT_UNROLL + c) * M, M), M)
            pq_ref[rows, :] = jnp.dot(u_ref[0, rows, :], g_ref[...], preferred_element_type=f32).astype(bf16)
        return carry

    lax.fori_loop(0, RES // DFT_UNROLL, channel_body, 0)

    rb = 16

    def butterfly_body(i, carry):
        row = pl.multiple_of(i * rb, rb)
        tw = tw_ref[pl.ds(row, rb), :]
        for half in range(D_FOURIER // LANES):
            lanes = slice(half * LANES, (half + 1) * LANES)
            zs = []
            for s2 in range(RES):
                rows = pl.ds(pl.multiple_of(s2 * M + row, rb), rb)
                zs.append((pq_ref[rows, lanes].astype(f32),
                           pq_ref[rows, slice(D_FOURIER + half * LANES, D_FOURIER + (half + 1) * LANES)].astype(f32)))
            ys = _fft_list(zs)
            for k2 in range(RES):
                re, im = ys[k2]
                if k2:
                    c = tw[:, k2:k2 + 1]
                    s = tw[:, RES + k2:RES + k2 + 1]
                    re, im = re * c - im * s, im * c + re * s
                br_ref[k2, pl.ds(row, rb), lanes] = re.astype(bf16)
                bi_ref[k2, pl.ds(row, rb), lanes] = im.astype(bf16)
        return carry

    lax.fori_loop(0, M // rb, butterfly_body, 0)

    def dft_body(i, carry):
        for u in range(DFT_UNROLL):
            k2 = i * DFT_UNROLL + u
            rhs = jnp.concatenate([br_ref[k2], bi_ref[k2]], axis=0)
            res = jnp.dot(d_ref[...], rhs, preferred_element_type=f32)
            o_ref[0, k2] = (res * gf_ref[0, k2].astype(f32)).astype(bf16)
        return carry

    lax.fori_loop(0, RES // DFT_UNROLL, dft_body, 0)


def _fourier(u, gmat, gf16, tw, dmat):
    B, S, _ = u.shape
    M = S // RES
    seq_spec = pl.BlockSpec((1, S, D_FOURIER), lambda b: (b, 0, 0))
    res_spec = pl.BlockSpec((1, RES, M, D_FOURIER), lambda b: (b, 0, 0, 0))
    return pl.pallas_call(
        functools.partial(_fnet_kernel, M=M),
        grid=(B,),
        in_specs=[seq_spec, pl.BlockSpec((D_FOURIER, 2 * D_FOURIER), lambda b: (0, 0)), res_spec,
                  pl.BlockSpec((M, LANES), lambda b: (0, 0)),
                  pl.BlockSpec((M, 2 * M), lambda b: (0, 0))],
        out_specs=res_spec,
        out_shape=jax.ShapeDtypeStruct((B, RES, M, D_FOURIER), bf16),
        scratch_shapes=[pltpu.VMEM((S, 2 * D_FOURIER), bf16)] + [pltpu.VMEM((RES, M, D_FOURIER), bf16)] * 2,
        compiler_params=pltpu.CompilerParams(
            dimension_semantics=("parallel",), vmem_limit_bytes=VMEM_LIMIT),
        name="fnet",
    )(u, gmat, gf16, tw, dmat)


def _outproj_kernel(a_ref, f_ref, x_ref, w_ref, y_ref, scr_ref, *, tmm):
    tm = RES * tmm
    attn = jnp.concatenate([a_ref[0, j].reshape(tm, LANES) for j in range(N_PAIRS)], axis=-1)
    mix = jnp.concatenate([attn, f_ref[0].reshape(tm, D_FOURIER)], axis=-1)
    y = jnp.dot(mix, w_ref[...], preferred_element_type=f32)
    for c in range(D_MODEL // LANES):
        for r in range(RES):
            scr_ref[c, pl.ds(r, tmm, stride=PITCH), :] = y[r * tmm:(r + 1) * tmm, c * LANES:(c + 1) * LANES]
    for c in range(D_MODEL // LANES):
        cols = slice(c * LANES, (c + 1) * LANES)
        for g in range(tmm):
            rows = slice(g * RES, (g + 1) * RES)
            y_ref[0, rows, cols] = x_ref[0, rows, cols] + scr_ref[c, g * PITCH:g * PITCH + RES, :]


def _outproj(attn16, four16, x, w_out, tmm=64):
    B, S, _ = x.shape
    M = S // RES
    tm = RES * tmm
    x_spec = pl.BlockSpec((1, tm, D_MODEL), lambda b, i: (b, i, 0))
    return pl.pallas_call(
        functools.partial(_outproj_kernel, tmm=tmm),
        grid=(B, M // tmm),
        in_specs=[pl.BlockSpec((1, N_PAIRS, RES, tmm, LANES), lambda b, i: (b, 0, 0, i, 0)),
                  pl.BlockSpec((1, RES, tmm, D_FOURIER), lambda b, i: (b, 0, i, 0)),
                  x_spec,
                  pl.BlockSpec((D_MODEL, D_MODEL), lambda b, i: (0, 0))],
        out_specs=x_spec,
        out_shape=jax.ShapeDtypeStruct((B, S, D_MODEL), f32),
        scratch_shapes=[pltpu.VMEM((D_MODEL // LANES, tmm * PITCH, LANES), f32)],
        compiler_params=pltpu.CompilerParams(
            dimension_semantics=("parallel", "parallel"), vmem_limit_bytes=VMEM_LIMIT),
        name="outproj",
    )(attn16, four16, x, w_out)


def _encoder_layer(x, norm_g, w_in, qg, kg, bd, bias, headmask, w_out, gmat):
    tw, dmat = _dft_tables(x.shape[1])
    q16, k16, v16, ga16, gf16, kn, vn, u = _inproj(x, norm_g, w_in, qg, kg, bd)
    attn16 = _attention(q16, k16, v16, kn, vn, ga16, bias, headmask)
    four16 = _fourier(u, gmat, gf16, jnp.asarray(tw), jnp.asarray(dmat, bf16))
    return _outproj(attn16, four16, x, w_out)


def _layer_params(norm_g, w_in, q_norm_g, k_norm_g, rel_bias, w_four, w_out):
    bias = _bias_tiles(rel_bias)
    headmask = jnp.asarray(np.stack([np.arange(LANES) < HEAD_DIM, np.arange(LANES) >= HEAD_DIM]), bf16)
    bd = jnp.asarray(np.kron(np.eye(MXU // HEAD_DIM), np.full((HEAD_DIM, HEAD_DIM), 1.0 / HEAD_DIM)), bf16)
    w_in_bf, w_out_bf, gmat = _prepare_weights(w_in, w_four, w_out)
    qg = jnp.tile(q_norm_g, N_HEADS)[None, :] * (LOG2E / math.sqrt(HEAD_DIM))
    kg = jnp.tile(k_norm_g, N_HEADS)[None, :]
    return (norm_g[None, :], w_in_bf, qg, kg, bd, bias, headmask, w_out_bf, gmat)


def kernel(x_prompt, x_sample, norm_g, w_in, q_norm_g, k_norm_g, rel_bias, w_four, w_out):
    y_prompt, y_sample = x_prompt, x_sample
    for layer in range(norm_g.shape[0]):
        args = _layer_params(norm_g[layer], w_in[layer], q_norm_g[layer], k_norm_g[layer], rel_bias,
                             w_four[layer], w_out[layer])
        y_prompt = _encoder_layer(y_prompt, *args)
        y_sample = _encoder_layer(y_sample, *args)
    return (y_prompt, y_sample)
```

```python
import functools
import math

import numpy as np
import jax
import jax.numpy as jnp
from jax import lax
from jax.experimental import pallas as pl
from jax.experimental.pallas import tpu as pltpu

D_MODEL = 1024
HEAD_DIM = 64
N_HEADS = 12
N_PAIRS = N_HEADS // 2
D_ATTN = N_HEADS * HEAD_DIM
N_GROUPS = 4
GROUP = 64
D_FOURIER = N_GROUPS * GROUP
N_BUCKETS = 32
MAX_DISTANCE = 1024
RMS_EPS = 1e-6
NEG = -1e30
RES = 16
HALF = 64
TQ = 128
TK = 256
D4_RES = 4
MXU = 256
MAX_SUBTILES = 4
DFT_UNROLL = 8
LOG2E = math.log2(math.e)
LANES = 128
X_BUFFERS = 3
PITCH = 24
VMEM_LIMIT = 52 * 1024 * 1024

C_Q, C_K, C_V, C_GA, C_U, C_GF, C_END = 0, 768, 1536, 2304, 3072, 3328, 3584

f32 = jnp.float32
bf16 = jnp.bfloat16


def _t5_bucket_np(rel):
    half = N_BUCKETS // 2
    max_exact = half // 2
    ret = np.where(rel > 0, half, 0)
    n = np.abs(rel)
    nf = np.maximum(n, 1).astype(np.float32)
    large = max_exact + (np.log(nf / np.float32(max_exact)) / np.float32(math.log(MAX_DISTANCE / max_exact))
                         * np.float32(half - max_exact)).astype(np.int32)
    large = np.minimum(large, half - 1)
    return ret + np.where(n < max_exact, n, large)


def _bucket_tiles():
    tiles = np.zeros((3, 3, TQ, TK), np.int32)
    i = np.arange(TQ)[:, None]
    c = np.arange(TK)[None, :]
    for var in range(3):
        rel = c - HALF * var - (RES * (i % (TQ // RES)) + i // (TQ // RES))
        tiles[0, var] = np.where(np.abs(rel) <= HALF, _t5_bucket_np(rel * 1), -1)
        qr, kr = TQ // D4_RES, TK // D4_RES
        rel = D4_RES * ((c % kr) - (i % qr) - (HALF // D4_RES) * var) + (c // kr - i // qr)
        tiles[1, var] = np.where(np.abs(rel) <= HALF, _t5_bucket_np(rel * D4_RES), -1)
        rel = c - HALF * var - i
        tiles[2, var] = np.where(np.abs(rel) <= HALF, _t5_bucket_np(rel * RES), -1)
    return tiles.reshape(9, TQ, TK)


def _dft_tables(seq):
    m = seq // RES
    s1 = np.arange(m)[:, None]
    k2 = np.arange(RES)[None, :]
    ang = 2.0 * np.pi * ((s1 * k2) % seq) / seq
    tw = np.zeros((m, LANES), np.float32)
    tw[:, :RES] = np.cos(ang)
    tw[:, RES:2 * RES] = np.sin(ang)
    k1 = np.arange(m)[:, None]
    ang = 2.0 * np.pi * ((k1 * np.arange(m)[None, :]) % m) / m
    scale = 1.0 / math.sqrt(seq)
    dmat = np.concatenate([np.cos(ang) * scale, -np.sin(ang) * scale], axis=1)
    return tw, dmat.astype(np.float32)


def _channel_dft_blockdiag():
    c = np.arange(GROUP)
    ang = 2.0 * np.pi * ((c[:, None] * c[None, :]) % GROUP) / GROUP
    eye = np.eye(N_GROUPS)
    scale = 1.0 / math.sqrt(GROUP)
    return (np.kron(eye, np.cos(ang) * scale).astype(np.float32),
            np.kron(eye, np.sin(ang) * scale).astype(np.float32))


def _bias_kernel(rbt_ref, idx_ref, out_ref):
    lane = lax.broadcasted_iota(jnp.int32, (1, LANES), 1)
    for head in range(N_HEADS):
        row = jnp.where(lane == N_BUCKETS, NEG, rbt_ref[head:head + 1, :] * LOG2E)
        table = jnp.broadcast_to(row, (TQ, LANES))
        for t in range(idx_ref.shape[0]):
            for c in range(TK // LANES):
                cols = slice(c * LANES, (c + 1) * LANES)
                out_ref[t, head // 2, (head % 2) * TQ:(head % 2 + 1) * TQ, cols] = jnp.take_along_axis(
                    table, idx_ref[t, :, cols], axis=1)


def _bias_tiles(rel_bias):
    idx = _bucket_tiles()
    idx = np.where(idx < 0, N_BUCKETS, idx).astype(np.int32)
    rbt = jnp.pad(rel_bias.T, ((0, 0), (0, LANES - N_BUCKETS)))
    return pl.pallas_call(
        _bias_kernel,
        out_shape=jax.ShapeDtypeStruct((9, N_PAIRS, 2 * TQ, TK), f32),
        compiler_params=pltpu.CompilerParams(vmem_limit_bytes=VMEM_LIMIT),
        name="bias_tiles",
    )(rbt, jnp.asarray(idx))


def _weights_kernel(win_ref, wout_ref, ww_ref, cbd_ref, sbd_ref, winb_ref, woutb_ref, g_ref):
    hi = lax.Precision.HIGHEST
    winb_ref[...] = win_ref[...].astype(bf16)
    woutb_ref[...] = wout_ref[...].astype(bf16)
    g_ref[:, :D_FOURIER] = jnp.dot(cbd_ref[...], ww_ref[...], precision=hi, preferred_element_type=f32).astype(bf16)
    g_ref[:, D_FOURIER:] = jnp.dot(sbd_ref[...], ww_ref[...], precision=hi, preferred_element_type=f32).astype(bf16)


def _prepare_weights(w_in, w_four, w_out, row_block=256):
    cbd, sbd = _channel_dft_blockdiag()
    blockmask = jnp.asarray(np.kron(np.eye(N_GROUPS), np.ones((GROUP, GROUP))).astype(np.float32))
    wwide = jnp.tile(w_four.reshape(D_FOURIER, GROUP), (1, N_GROUPS)) * blockmask
    rows = lambda width: pl.BlockSpec((row_block, width), lambda i: (i, 0))
    small = pl.BlockSpec((D_FOURIER, D_FOURIER), lambda i: (0, 0))
    return pl.pallas_call(
        _weights_kernel,
        grid=(D_MODEL // row_block,),
        in_specs=[rows(C_END), rows(D_MODEL), small, small, small],
        out_specs=[rows(C_END), rows(D_MODEL), pl.BlockSpec((D_FOURIER, 2 * D_FOURIER), lambda i: (0, 0))],
        out_shape=(jax.ShapeDtypeStruct((D_MODEL, C_END), bf16),
                   jax.ShapeDtypeStruct((D_MODEL, D_MODEL), bf16),
                   jax.ShapeDtypeStruct((D_FOURIER, 2 * D_FOURIER), bf16)),
        compiler_params=pltpu.CompilerParams(dimension_semantics=("arbitrary",)),
        name="prepare_weights",
    )(w_in, w_out, wwide, jnp.asarray(cbd), jnp.asarray(sbd))


def _silu(x):
    return x * (1.0 / (1.0 + jnp.exp(-x)))


def _inproj_kernel(x_ref, g_ref, w_ref, qg_ref, kg_ref, bd_ref,
                   q16_ref, k16_ref, v16_ref, ga16_ref, gf16_ref, kn_ref, vn_ref, u_ref,
                   scr_ref, *, tmm):
    xs = x_ref[0]
    ms = jnp.mean(xs * xs, axis=-1, keepdims=True)
    h = (xs * lax.rsqrt(ms + RMS_EPS) * g_ref[...]).astype(bf16)

    def proj(lo, hi):
        return jnp.dot(h, w_ref[:, lo:hi], preferred_element_type=f32)

    def head_norm(t, gain):
        t2 = (t * t).astype(bf16)
        msq = jnp.concatenate(
            [jnp.dot(t2[:, c * MXU:(c + 1) * MXU], bd_ref[...], preferred_element_type=f32)
             for c in range(D_ATTN // MXU)], axis=-1)
        return t * lax.rsqrt(msq + RMS_EPS) * gain

    def residue_rows(slab, val):
        for g in range(tmm):
            scr_ref[slab, g * PITCH:g * PITCH + RES, :] = val[g * RES:(g + 1) * RES]
        return [scr_ref[slab, pl.ds(r, tmm, stride=PITCH), :] for r in range(RES)]

    def put_res(ref, val, first_slab):
        for j in range(N_PAIRS):
            rows = residue_rows(first_slab + j, val[:, j * LANES:(j + 1) * LANES])
            for r in range(RES):
                ref[0, j, r] = rows[r].astype(bf16)

    def put_nat(ref, val):
        for j in range(N_PAIRS):
            ref[0, j] = val[:, j * LANES:(j + 1) * LANES].astype(bf16)

    qn = head_norm(proj(C_Q, C_K), qg_ref[...])
    put_res(q16_ref, qn, 0)
    kn = head_norm(proj(C_K, C_V), kg_ref[...])
    put_res(k16_ref, kn, 6)
    put_nat(kn_ref, kn)
    v = proj(C_V, C_GA)
    put_res(v16_ref, v, 12)
    put_nat(vn_ref, v)
    put_res(ga16_ref, _silu(proj(C_GA, C_U)), 18)
    half = h.shape[0] // 2
    four = jnp.concatenate(
        [jnp.dot(h[:half], w_ref[:, C_U:C_END], preferred_element_type=f32),
         jnp.dot(h[half:], w_ref[:, C_U:C_END], preferred_element_type=f32)], axis=0)
    u_ref[0] = four[:, :D_FOURIER].astype(bf16)
    gf = _silu(four[:, D_FOURIER:])
    for c in range(D_FOURIER // LANES):
        rows = residue_rows(24 + c, gf[:, c * LANES:(c + 1) * LANES])
        for r in range(RES):
            gf16_ref[0, r, :, c * LANES:(c + 1) * LANES] = rows[r].astype(bf16)


def _inproj(x, norm_g, w_in, qg, kg, bd, tmm=32):
    B, S, _ = x.shape
    M = S // RES
    tm = RES * tmm
    res_shape = jax.ShapeDtypeStruct((B, N_PAIRS, RES, M, LANES), bf16)
    nat_shape = jax.ShapeDtypeStruct((B, N_PAIRS, S, LANES), bf16)
    four_shape = jax.ShapeDtypeStruct((B, S, D_FOURIER), bf16)
    res_spec = pl.BlockSpec((1, N_PAIRS, RES, tmm, LANES), lambda b, i: (b, 0, 0, i, 0))
    nat_spec = pl.BlockSpec((1, N_PAIRS, tm, LANES), lambda b, i: (b, 0, i, 0))
    four_spec = pl.BlockSpec((1, tm, D_FOURIER), lambda b, i: (b, i, 0))
    const = lambda shape: pl.BlockSpec(shape, lambda b, i: (0,) * len(shape))
    n_slabs = (4 * D_ATTN + D_FOURIER) // LANES
    return pl.pallas_call(
        functools.partial(_inproj_kernel, tmm=tmm),
        grid=(B, M // tmm),
        in_specs=[pl.BlockSpec((1, tm, D_MODEL), lambda b, i: (b, i, 0)),
                  const((1, D_MODEL)), const((D_MODEL, C_END)),
                  const((1, D_ATTN)), const((1, D_ATTN)), const((MXU, MXU))],
        out_specs=[res_spec, res_spec, res_spec, res_spec,
                   pl.BlockSpec((1, RES, tmm, D_FOURIER), lambda b, i: (b, 0, i, 0)),
                   nat_spec, nat_spec, four_spec],
        out_shape=[res_shape, res_shape, res_shape, res_shape,
                   jax.ShapeDtypeStruct((B, RES, M, D_FOURIER), bf16),
                   nat_shape, nat_shape, four_shape],
        scratch_shapes=[pltpu.VMEM((n_slabs, tmm * PITCH, LANES), f32)],
        compiler_params=pltpu.CompilerParams(
            dimension_semantics=("parallel", "parallel"), vmem_limit_bytes=VMEM_LIMIT),
        name="inproj",
    )(x, norm_g, w_in, qg, kg, bd)


def _attn_kernel(q_ref, k16_ref, v16_ref, kn_ref, vn_ref, ga_ref, bias_ref, hm_ref, o_ref, *, M, S, nsub, nbatch):
    is_a = lax.broadcasted_iota(jnp.int32, (TQ, LANES), 1) < HEAD_DIM
    mask_a = hm_ref[0:1, :]
    mask_b = hm_ref[1:2, :]
    ones = jnp.ones((TK, LANES), bf16)

    def tile(q, k, v, bias):
        qs = jnp.concatenate([q * mask_a, q * mask_b], axis=0)
        s = lax.dot_general(qs, k, (((1,), (1,)), ((), ())), preferred_element_type=f32) + bias
        mx = jnp.max(s, axis=-1, keepdims=True)
        p = jnp.exp2((s - mx).astype(bf16))
        pv = jnp.dot(p, jnp.concatenate([v, ones], axis=1), preferred_element_type=f32)
        o = jnp.where(is_a, pv[:TQ, :LANES], pv[TQ:, :LANES])
        lrep = jnp.where(is_a, pv[:TQ, LANES:], pv[TQ:, LANES:])
        mrep = jnp.where(is_a, mx[:TQ], mx[TQ:])
        return mrep, lrep, o

    def merge(old, new):
        (m_old, l_old, acc_old), (mr, lr, o) = old, new
        m_new = jnp.maximum(m_old, mr)
        alpha = jnp.exp2(m_old - m_new)
        beta = jnp.exp2(mr - m_new)
        return m_new, l_old * alpha + lr * beta, acc_old * alpha + o * beta

    def rows(stat, lo, n):
        return tuple(x[lo:lo + n] for x in stat)

    def cat(stats):
        return tuple(jnp.concatenate(xs, axis=0) for xs in zip(*stats))

    for bi, sub in [(bi, sub) for bi in range(nbatch) for sub in range(nsub)]:
        base = sub * TQ
        m0 = (pl.program_id(2) * nsub + sub) * TQ

        st1 = [[None] * (TQ // 8) for _ in range(RES)]
        for jj2 in range(TQ // 16):
            qf = q_ref[bi, 0, :, base + 16 * jj2:base + 16 * jj2 + 16, :].astype(f32)
            for par in range(2):
                jj = 2 * jj2 + par
                q = qf[:, 8 * par:8 * par + 8, :].reshape(TQ, LANES).astype(bf16)
                t0 = RES * (m0 + 8 * jj)
                ks = pl.multiple_of(jnp.clip(t0 - HALF, 0, S - TK), HALF)
                stat = tile(q, kn_ref[bi, 0, pl.ds(ks, TK), :], vn_ref[bi, 0, pl.ds(ks, TK), :],
                            bias_ref[(t0 - ks) // HALF, 0])
                for r in range(RES):
                    st1[r][jj] = rows(stat, 8 * r, 8)

        qr, kr, side = TQ // D4_RES, TK // D4_RES, HALF // D4_RES
        st4 = [[None] * D4_RES for _ in range(RES)]
        for b in range(D4_RES):
            for j4 in range(TQ // qr):
                ml = base + qr * j4
                mg = m0 + qr * j4
                ks = pl.multiple_of(jnp.clip(mg - side, 0, M - kr), side)
                res4 = [D4_RES * a + b for a in range(RES // D4_RES)]
                q = jnp.concatenate([q_ref[bi, 0, r, ml:ml + qr, :] for r in res4], axis=0)
                k = jnp.concatenate([k16_ref[bi, 0, r, pl.ds(ks, kr), :] for r in res4], axis=0)
                v = jnp.concatenate([v16_ref[bi, 0, r, pl.ds(ks, kr), :] for r in res4], axis=0)
                old = cat([st1[r][(qr // 8) * j4 + c] for r in res4 for c in range(qr // 8)])
                stat = merge(old, tile(q, k, v, bias_ref[3 + (mg - ks) // side, 0]))
                for a, r in enumerate(res4):
                    st4[r][j4] = rows(stat, qr * a, qr)

        ks = pl.multiple_of(jnp.clip(m0 - HALF, 0, M - TK), HALF)
        for r in range(RES):
            new = tile(q_ref[bi, 0, r, base:base + TQ, :], k16_ref[bi, 0, r, pl.ds(ks, TK), :],
                       v16_ref[bi, 0, r, pl.ds(ks, TK), :], bias_ref[6 + (m0 - ks) // HALF, 0])
            _, l_new, acc_new = merge(cat(st4[r]), new)
            gate = ga_ref[bi, 0, r, base:base + TQ, :].astype(f32)
            o_ref[bi, 0, r, base:base + TQ, :] = (acc_new / l_new * gate).astype(bf16)


def _attention(q16, k16, v16, kn, vn, ga16, bias, headmask):
    B, _, _, M, _ = q16.shape
    S = M * RES
    nsub = min(MAX_SUBTILES, M // TQ)
    rows = nsub * TQ
    nbatch = math.gcd(B, MAX_SUBTILES // nsub)
    tile_spec = pl.BlockSpec((nbatch, 1, RES, rows, LANES), lambda j, b, t: (b, j, 0, t, 0))
    seq16_spec = pl.BlockSpec((nbatch, 1, RES, M, LANES), lambda j, b, t: (b, j, 0, 0, 0))
    nat_spec = pl.BlockSpec((nbatch, 1, S, LANES), lambda j, b, t: (b, j, 0, 0))
    return pl.pallas_call(
        functools.partial(_attn_kernel, M=M, S=S, nsub=nsub, nbatch=nbatch),
        grid=(N_PAIRS, B // nbatch, M // rows),
        in_specs=[tile_spec, seq16_spec, seq16_spec, nat_spec, nat_spec, tile_spec,
                  pl.BlockSpec((9, 1, 2 * TQ, TK), lambda j, b, t: (0, j, 0, 0)),
                  pl.BlockSpec((2, LANES), lambda j, b, t: (0, 0))],
        out_specs=tile_spec,
        out_shape=jax.ShapeDtypeStruct((B, N_PAIRS, RES, M, LANES), bf16),
        compiler_params=pltpu.CompilerParams(
            dimension_semantics=("parallel", "parallel", "arbitrary"), vmem_limit_bytes=VMEM_LIMIT),
        name="attn",
    )(q16, k16, v16, kn, vn, ga16, bias, headmask)


def _cmul_const(z, k, n):
    re, im = z
    if k == 0:
        return z
    if 4 * k == n:
        return -im, re
    c = math.cos(2.0 * math.pi * k / n)
    s = math.sin(2.0 * math.pi * k / n)
    return re * c - im * s, im * c + re * s


def _fft_list(xs):
    n = len(xs)
    if n == 1:
        return xs
    even = _fft_list(xs[0::2])
    odd = _fft_list(xs[1::2])
    out = [None] * n
    for k in range(n // 2):
        t = _cmul_const(odd[k], k, n)
        out[k] = (even[k][0] + t[0], even[k][1] + t[1])
        out[k + n // 2] = (even[k][0] - t[0], even[k][1] - t[1])
    return out


def _fnet_kernel(u_ref, g_ref, gf_ref, tw_ref, d_ref, o_ref, pq_ref, br_ref, bi_ref, *, M):
    def channel_body(i, carry):
        for c in range(DFT_UNROLL):
            rows = pl.ds(pl.multiple_of((i * DFT_UNROLL + c) * M, M), M)
            pq_ref[rows, :] = jnp.dot(u_ref[0, rows, :], g_ref[...], preferred_element_type=f32).astype(bf16)
        return carry

    lax.fori_loop(0, RES // DFT_UNROLL, channel_body, 0)

    rb = 16

    def butterfly_body(i, carry):
        row = pl.multiple_of(i * rb, rb)
        tw = tw_ref[pl.ds(row, rb), :]
        for half in range(D_FOURIER // LANES):
            lanes = slice(half * LANES, (half + 1) * LANES)
            zs = []
            for s2 in range(RES):
                rows = pl.ds(pl.multiple_of(s2 * M + row, rb), rb)
                zs.append((pq_ref[rows, lanes].astype(f32),
                           pq_ref[rows, slice(D_FOURIER + half * LANES, D_FOURIER + (half + 1) * LANES)].astype(f32)))
            ys = _fft_list(zs)
            for k2 in range(RES):
                re, im = ys[k2]
                if k2:
                    c = tw[:, k2:k2 + 1]
                    s = tw[:, RES + k2:RES + k2 + 1]
                    re, im = re * c - im * s, im * c + re * s
                br_ref[k2, pl.ds(row, rb), lanes] = re.astype(bf16)
                bi_ref[k2, pl.ds(row, rb), lanes] = im.astype(bf16)
        return carry

    lax.fori_loop(0, M // rb, butterfly_body, 0)

    def dft_body(i, carry):
        for u in range(DFT_UNROLL):
            k2 = i * DFT_UNROLL + u
            rhs = jnp.concatenate([br_ref[k2], bi_ref[k2]], axis=0)
            res = jnp.dot(d_ref[...], rhs, preferred_element_type=f32)
            o_ref[0, k2] = (res * gf_ref[0, k2].astype(f32)).astype(bf16)
        return carry

    lax.fori_loop(0, RES // DFT_UNROLL, dft_body, 0)


def _fourier(u, gmat, gf16, tw, dmat):
    B, S, _ = u.shape
    M = S // RES
    seq_spec = pl.BlockSpec((1, S, D_FOURIER), lambda b: (b, 0, 0))
    res_spec = pl.BlockSpec((1, RES, M, D_FOURIER), lambda b: (b, 0, 0, 0))
    return pl.pallas_call(
        functools.partial(_fnet_kernel, M=M),
        grid=(B,),
        in_specs=[seq_spec, pl.BlockSpec((D_FOURIER, 2 * D_FOURIER), lambda b: (0, 0)), res_spec,
                  pl.BlockSpec((M, LANES), lambda b: (0, 0)),
                  pl.BlockSpec((M, 2 * M), lambda b: (0, 0))],
        out_specs=res_spec,
        out_shape=jax.ShapeDtypeStruct((B, RES, M, D_FOURIER), bf16),
        scratch_shapes=[pltpu.VMEM((S, 2 * D_FOURIER), bf16)] + [pltpu.VMEM((RES, M, D_FOURIER), bf16)] * 2,
        compiler_params=pltpu.CompilerParams(
            dimension_semantics=("parallel",), vmem_limit_bytes=VMEM_LIMIT),
        name="fnet",
    )(u, gmat, gf16, tw, dmat)


def _outproj_kernel(a_ref, f_ref, x_hbm, w_ref, y_ref, scr_ref, xbuf_ref, sem_ref, *, tmm, steps, total):
    tm = RES * tmm
    t = pl.program_id(0) * steps + pl.program_id(1)

    def x_copy(step):
        rows = pl.ds(pl.multiple_of((step % steps) * tm, tm), tm)
        slot = step % X_BUFFERS
        return pltpu.make_async_copy(x_hbm.at[step // steps, rows, :], xbuf_ref.at[slot], sem_ref.at[slot])

    @pl.when(t == 0)
    def _():
        for k in range(X_BUFFERS - 1):
            x_copy(k).start()

    @pl.when(t + X_BUFFERS - 1 < total)
    def _():
        x_copy(t + X_BUFFERS - 1).start()

    x_copy(t).wait()
    x_ref = xbuf_ref.at[t % X_BUFFERS]
    attn = jnp.concatenate([a_ref[0, j].reshape(tm, LANES) for j in range(N_PAIRS)], axis=-1)
    mix = jnp.concatenate([attn, f_ref[0].reshape(tm, D_FOURIER)], axis=-1)
    y = jnp.dot(mix, w_ref[...], preferred_element_type=f32)
    for c in range(D_MODEL // LANES):
        for r in range(RES):
            scr_ref[c, pl.ds(r, tmm, stride=PITCH), :] = y[r * tmm:(r + 1) * tmm, c * LANES:(c + 1) * LANES]
    for c in range(D_MODEL // LANES):
        cols = slice(c * LANES, (c + 1) * LANES)
        for g in range(tmm):
            rows = slice(g * RES, (g + 1) * RES)
            y_ref[0, rows, cols] = x_ref[rows, cols] + scr_ref[c, g * PITCH:g * PITCH + RES, :]


def _outproj(attn16, four16, x, w_out, tmm=64):
    B, S, _ = x.shape
    M = S // RES
    tm = RES * tmm
    steps = M // tmm
    return pl.pallas_call(
        functools.partial(_outproj_kernel, tmm=tmm, steps=steps, total=B * steps),
        grid=(B, steps),
        in_specs=[pl.BlockSpec((1, N_PAIRS, RES, tmm, LANES), lambda b, i: (b, 0, 0, i, 0)),
                  pl.BlockSpec((1, RES, tmm, D_FOURIER), lambda b, i: (b, 0, i, 0)),
                  pl.BlockSpec(memory_space=pl.ANY),
                  pl.BlockSpec((D_MODEL, D_MODEL), lambda b, i: (0, 0))],
        out_specs=pl.BlockSpec((1, tm, D_MODEL), lambda b, i: (b, i, 0)),
        out_shape=jax.ShapeDtypeStruct((B, S, D_MODEL), f32),
        scratch_shapes=[pltpu.VMEM((D_MODEL // LANES, tmm * PITCH, LANES), f32),
                        pltpu.VMEM((X_BUFFERS, tm, D_MODEL), f32),
                        pltpu.SemaphoreType.DMA((X_BUFFERS,))],
        compiler_params=pltpu.CompilerParams(
            dimension_semantics=("arbitrary", "arbitrary"), vmem_limit_bytes=VMEM_LIMIT),
        name="outproj",
    )(attn16, four16, x, w_out)


def _encoder_layer(x, norm_g, w_in, qg, kg, bd, bias, headmask, w_out, gmat):
    tw, dmat = _dft_tables(x.shape[1])
    q16, k16, v16, ga16, gf16, kn, vn, u = _inproj(x, norm_g, w_in, qg, kg, bd)
    attn16 = _attention(q16, k16, v16, kn, vn, ga16, bias, headmask)
    four16 = _fourier(u, gmat, gf16, jnp.asarray(tw), jnp.asarray(dmat, bf16))
    return _outproj(attn16, four16, x, w_out)


def _layer_params(norm_g, w_in, q_norm_g, k_norm_g, rel_bias, w_four, w_out):
    bias = _bias_tiles(rel_bias)
    headmask = jnp.asarray(np.stack([np.arange(LANES) < HEAD_DIM, np.arange(LANES) >= HEAD_DIM]), bf16)
    bd = jnp.asarray(np.kron(np.eye(MXU // HEAD_DIM), np.full((HEAD_DIM, HEAD_DIM), 1.0 / HEAD_DIM)), bf16)
    w_in_bf, w_out_bf, gmat = _prepare_weights(w_in, w_four, w_out)
    qg = jnp.tile(q_norm_g, N_HEADS)[None, :] * (LOG2E / math.sqrt(HEAD_DIM))
    kg = jnp.tile(k_norm_g, N_HEADS)[None, :]
    return (norm_g[None, :], w_in_bf, qg, kg, bd, bias, headmask, w_out_bf, gmat)


def kernel(x_prompt, x_sample, norm_g, w_in, q_norm_g, k_norm_g, rel_bias, w_four, w_out):
    y_prompt, y_sample = x_prompt, x_sample
    for layer in range(norm_g.shape[0]):
        args = _layer_params(norm_g[layer], w_in[layer], q_norm_g[layer], k_norm_g[layer], rel_bias,
                             w_four[layer], w_out[layer])
        y_prompt = _encoder_layer(y_prompt, *args)
        y_sample = _encoder_layer(y_sample, *args)
    return (y_prompt, y_sample)
```

```python
import functools
import math

import numpy as np
import jax
import jax.numpy as jnp
from jax import lax
from jax.experimental import pallas as pl
from jax.experimental.pallas import tpu as pltpu

D_MODEL = 1024
HEAD_DIM = 64
N_HEADS = 12
N_PAIRS = N_HEADS // 2
D_ATTN = N_HEADS * HEAD_DIM
N_GROUPS = 4
GROUP = 64
D_FOURIER = N_GROUPS * GROUP
N_BUCKETS = 32
MAX_DISTANCE = 1024
RMS_EPS = 1e-6
NEG = -1e30
RES = 16
HALF = 64
TQ = 128
TK = 256
D4_RES = 4
MXU = 256
MAX_SUBTILES = 4
DFT_UNROLL = 8
LOG2E = math.log2(math.e)
LANES = 128
X_BUFFERS = 3
PITCH = 24
VMEM_LIMIT = 52 * 1024 * 1024

C_Q, C_K, C_V, C_GA, C_U, C_GF, C_END = 0, 768, 1536, 2304, 3072, 3328, 3584

f32 = jnp.float32
bf16 = jnp.bfloat16


def _t5_bucket_np(rel):
    half = N_BUCKETS // 2
    max_exact = half // 2
    ret = np.where(rel > 0, half, 0)
    n = np.abs(rel)
    nf = np.maximum(n, 1).astype(np.float32)
    large = max_exact + (np.log(nf / np.float32(max_exact)) / np.float32(math.log(MAX_DISTANCE / max_exact))
                         * np.float32(half - max_exact)).astype(np.int32)
    large = np.minimum(large, half - 1)
    return ret + np.where(n < max_exact, n, large)


def _bucket_tiles():
    tiles = np.zeros((3, 3, TQ, TK), np.int32)
    i = np.arange(TQ)[:, None]
    c = np.arange(TK)[None, :]
    for var in range(3):
        rel = c - HALF * var - (RES * (i % (TQ // RES)) + i // (TQ // RES))
        tiles[0, var] = np.where(np.abs(rel) <= HALF, _t5_bucket_np(rel * 1), -1)
        qr, kr = TQ // D4_RES, TK // D4_RES
        rel = D4_RES * ((c % kr) - (i % qr) - (HALF // D4_RES) * var) + (c // kr - i // qr)
        tiles[1, var] = np.where(np.abs(rel) <= HALF, _t5_bucket_np(rel * D4_RES), -1)
        rel = c - HALF * var - i
        tiles[2, var] = np.where(np.abs(rel) <= HALF, _t5_bucket_np(rel * RES), -1)
    return tiles.reshape(9, TQ, TK)


def _dft_tables(seq):
    m = seq // RES
    s1 = np.arange(m)[:, None]
    k2 = np.arange(RES)[None, :]
    ang = 2.0 * np.pi * ((s1 * k2) % seq) / seq
    tw = np.zeros((m, LANES), np.float32)
    tw[:, :RES] = np.cos(ang)
    tw[:, RES:2 * RES] = np.sin(ang)
    k1 = np.arange(m)[:, None]
    ang = 2.0 * np.pi * ((k1 * np.arange(m)[None, :]) % m) / m
    scale = 1.0 / math.sqrt(seq)
    dmat = np.concatenate([np.cos(ang) * scale, -np.sin(ang) * scale], axis=1)
    return tw, dmat.astype(np.float32)


def _channel_dft_blockdiag():
    c = np.arange(GROUP)
    ang = 2.0 * np.pi * ((c[:, None] * c[None, :]) % GROUP) / GROUP
    eye = np.eye(N_GROUPS)
    scale = 1.0 / math.sqrt(GROUP)
    return (np.kron(eye, np.cos(ang) * scale).astype(np.float32),
            np.kron(eye, np.sin(ang) * scale).astype(np.float32))


def _bias_kernel(rbt_ref, idx_ref, out_ref):
    lane = lax.broadcasted_iota(jnp.int32, (1, LANES), 1)
    for head in range(N_HEADS):
        row = jnp.where(lane == N_BUCKETS, NEG, rbt_ref[head:head + 1, :] * LOG2E)
        table = jnp.broadcast_to(row, (TQ, LANES))
        for t in range(idx_ref.shape[0]):
            for c in range(TK // LANES):
                cols = slice(c * LANES, (c + 1) * LANES)
                out_ref[t, head // 2, (head % 2) * TQ:(head % 2 + 1) * TQ, cols] = jnp.take_along_axis(
                    table, idx_ref[t, :, cols], axis=1)


def _bias_tiles(rel_bias):
    idx = _bucket_tiles()
    idx = np.where(idx < 0, N_BUCKETS, idx).astype(np.int32)
    rbt = jnp.pad(rel_bias.T, ((0, 0), (0, LANES - N_BUCKETS)))
    return pl.pallas_call(
        _bias_kernel,
        out_shape=jax.ShapeDtypeStruct((9, N_PAIRS, 2 * TQ, TK), f32),
        compiler_params=pltpu.CompilerParams(vmem_limit_bytes=VMEM_LIMIT),
        name="bias_tiles",
    )(rbt, jnp.asarray(idx))


def _weights_kernel(win_ref, wout_ref, ww_ref, cbd_ref, sbd_ref, winb_ref, woutb_ref, g_ref):
    hi = lax.Precision.HIGHEST
    winb_ref[...] = win_ref[...].astype(bf16)
    woutb_ref[...] = wout_ref[...].astype(bf16)
    g_ref[:, :D_FOURIER] = jnp.dot(cbd_ref[...], ww_ref[...], precision=hi, preferred_element_type=f32).astype(bf16)
    g_ref[:, D_FOURIER:] = jnp.dot(sbd_ref[...], ww_ref[...], precision=hi, preferred_element_type=f32).astype(bf16)


def _prepare_weights(w_in, w_four, w_out, row_block=256):
    cbd, sbd = _channel_dft_blockdiag()
    blockmask = jnp.asarray(np.kron(np.eye(N_GROUPS), np.ones((GROUP, GROUP))).astype(np.float32))
    wwide = jnp.tile(w_four.reshape(D_FOURIER, GROUP), (1, N_GROUPS)) * blockmask
    rows = lambda width: pl.BlockSpec((row_block, width), lambda i: (i, 0))
    small = pl.BlockSpec((D_FOURIER, D_FOURIER), lambda i: (0, 0))
    return pl.pallas_call(
        _weights_kernel,
        grid=(D_MODEL // row_block,),
        in_specs=[rows(C_END), rows(D_MODEL), small, small, small],
        out_specs=[rows(C_END), rows(D_MODEL), pl.BlockSpec((D_FOURIER, 2 * D_FOURIER), lambda i: (0, 0))],
        out_shape=(jax.ShapeDtypeStruct((D_MODEL, C_END), bf16),
                   jax.ShapeDtypeStruct((D_MODEL, D_MODEL), bf16),
                   jax.ShapeDtypeStruct((D_FOURIER, 2 * D_FOURIER), bf16)),
        compiler_params=pltpu.CompilerParams(dimension_semantics=("arbitrary",)),
        name="prepare_weights",
    )(w_in, w_out, wwide, jnp.asarray(cbd), jnp.asarray(sbd))


def _silu(x):
    return x * (1.0 / (1.0 + jnp.exp(-x)))


def _inproj_kernel(x_ref, g_ref, w_ref, qg_ref, kg_ref, bd_ref,
                   q16_ref, k16_ref, v16_ref, ga16_ref, gf16_ref, kn_ref, vn_ref, u_ref,
                   scr_ref, *, tmm):
    xs = x_ref[0]
    ms = jnp.mean(xs * xs, axis=-1, keepdims=True)
    h = (xs * lax.rsqrt(ms + RMS_EPS) * g_ref[...]).astype(bf16)

    def proj(lo, hi):
        return jnp.dot(h, w_ref[:, lo:hi], preferred_element_type=f32)

    def head_norm(t, gain):
        t2 = (t * t).astype(bf16)
        msq = jnp.concatenate(
            [jnp.dot(t2[:, c * MXU:(c + 1) * MXU], bd_ref[...], preferred_element_type=f32)
             for c in range(D_ATTN // MXU)], axis=-1)
        return t * lax.rsqrt(msq + RMS_EPS) * gain

    def residue_rows(slab, val):
        for g in range(tmm):
            scr_ref[slab, g * PITCH:g * PITCH + RES, :] = val[g * RES:(g + 1) * RES]
        return [scr_ref[slab, pl.ds(r, tmm, stride=PITCH), :] for r in range(RES)]

    def put_res(ref, val, first_slab):
        for j in range(N_PAIRS):
            rows = residue_rows(first_slab + j, val[:, j * LANES:(j + 1) * LANES])
            for r in range(RES):
                ref[0, j, r] = rows[r].astype(bf16)

    def put_nat(ref, val):
        for j in range(N_PAIRS):
            ref[0, j] = val[:, j * LANES:(j + 1) * LANES].astype(bf16)

    qn = head_norm(proj(C_Q, C_K), qg_ref[...])
    put_res(q16_ref, qn, 0)
    kn = head_norm(proj(C_K, C_V), kg_ref[...])
    put_res(k16_ref, kn, 6)
    put_nat(kn_ref, kn)
    v = proj(C_V, C_GA)
    put_res(v16_ref, v, 12)
    put_nat(vn_ref, v)
    put_res(ga16_ref, _silu(proj(C_GA, C_U)), 18)
    half = h.shape[0] // 2
    four = jnp.concatenate(
        [jnp.dot(h[:half], w_ref[:, C_U:C_END], preferred_element_type=f32),
         jnp.dot(h[half:], w_ref[:, C_U:C_END], preferred_element_type=f32)], axis=0)
    u_ref[0] = four[:, :D_FOURIER].astype(bf16)
    gf = _silu(four[:, D_FOURIER:])
    for c in range(D_FOURIER // LANES):
        rows = residue_rows(24 + c, gf[:, c * LANES:(c + 1) * LANES])
        for r in range(RES):
            gf16_ref[0, r, :, c * LANES:(c + 1) * LANES] = rows[r].astype(bf16)


def _inproj(x, norm_g, w_in, qg, kg, bd, tmm=32):
    B, S, _ = x.shape
    M = S // RES
    tm = RES * tmm
    res_shape = jax.ShapeDtypeStruct((B, N_PAIRS, RES, M, LANES), bf16)
    nat_shape = jax.ShapeDtypeStruct((B, N_PAIRS, S, LANES), bf16)
    four_shape = jax.ShapeDtypeStruct((B, S, D_FOURIER), bf16)
    res_spec = pl.BlockSpec((1, N_PAIRS, RES, tmm, LANES), lambda b, i: (b, 0, 0, i, 0))
    nat_spec = pl.BlockSpec((1, N_PAIRS, tm, LANES), lambda b, i: (b, 0, i, 0))
    four_spec = pl.BlockSpec((1, tm, D_FOURIER), lambda b, i: (b, i, 0))
    const = lambda shape: pl.BlockSpec(shape, lambda b, i: (0,) * len(shape))
    n_slabs = (4 * D_ATTN + D_FOURIER) // LANES
    return pl.pallas_call(
        functools.partial(_inproj_kernel, tmm=tmm),
        grid=(B, M // tmm),
        in_specs=[pl.BlockSpec((1, tm, D_MODEL), lambda b, i: (b, i, 0)),
                  const((1, D_MODEL)), const((D_MODEL, C_END)),
                  const((1, D_ATTN)), const((1, D_ATTN)), const((MXU, MXU))],
        out_specs=[res_spec, res_spec, res_spec, res_spec,
                   pl.BlockSpec((1, RES, tmm, D_FOURIER), lambda b, i: (b, 0, i, 0)),
                   nat_spec, nat_spec, four_spec],
        out_shape=[res_shape, res_shape, res_shape, res_shape,
                   jax.ShapeDtypeStruct((B, RES, M, D_FOURIER), bf16),
                   nat_shape, nat_shape, four_shape],
        scratch_shapes=[pltpu.VMEM((n_slabs, tmm * PITCH, LANES), f32)],
        compiler_params=pltpu.CompilerParams(
            dimension_semantics=("parallel", "parallel"), vmem_limit_bytes=VMEM_LIMIT),
        name="inproj",
    )(x, norm_g, w_in, qg, kg, bd)


def _attn_kernel(q_ref, k16_ref, v16_ref, kn_ref, vn_ref, ga_ref, bias_ref, hm_ref, o_ref, *, M, S, nsub, nbatch):
    is_a = lax.broadcasted_iota(jnp.int32, (TQ, LANES), 1) < HEAD_DIM
    mask_a = hm_ref[0:1, :]
    mask_b = hm_ref[1:2, :]
    ones = jnp.ones((TK, LANES), bf16)

    def tile(q, k, v, bias):
        qs = jnp.concatenate([q * mask_a, q * mask_b], axis=0)
        s = lax.dot_general(qs, k, (((1,), (1,)), ((), ())), preferred_element_type=f32) + bias
        mx = jnp.max(s, axis=-1, keepdims=True)
        p = jnp.exp2((s - mx).astype(bf16))
        pv = jnp.dot(p, jnp.concatenate([v, ones], axis=1), preferred_element_type=f32)
        o = jnp.where(is_a, pv[:TQ, :LANES], pv[TQ:, :LANES])
        lrep = jnp.where(is_a, pv[:TQ, LANES:], pv[TQ:, LANES:])
        mrep = jnp.where(is_a, mx[:TQ], mx[TQ:])
        return mrep, lrep, o

    def merge(old, new):
        (m_old, l_old, acc_old), (mr, lr, o) = old, new
        m_new = jnp.maximum(m_old, mr)
        alpha = jnp.exp2(m_old - m_new)
        beta = jnp.exp2(mr - m_new)
        return m_new, l_old * alpha + lr * beta, acc_old * alpha + o * beta

    def rows(stat, lo, n):
        return tuple(x[lo:lo + n] for x in stat)

    def cat(stats):
        return tuple(jnp.concatenate(xs, axis=0) for xs in zip(*stats))

    for bi, sub in [(bi, sub) for bi in range(nbatch) for sub in range(nsub)]:
        base = sub * TQ
        m0 = (pl.program_id(2) * nsub + sub) * TQ

        st1 = [[None] * (TQ // 8) for _ in range(RES)]
        for jj2 in range(TQ // 16):
            qf = q_ref[bi, 0, :, base + 16 * jj2:base + 16 * jj2 + 16, :].astype(f32)
            for par in range(2):
                jj = 2 * jj2 + par
                q = qf[:, 8 * par:8 * par + 8, :].reshape(TQ, LANES).astype(bf16)
                t0 = RES * (m0 + 8 * jj)
                ks = pl.multiple_of(jnp.clip(t0 - HALF, 0, S - TK), HALF)
                stat = tile(q, kn_ref[bi, 0, pl.ds(ks, TK), :], vn_ref[bi, 0, pl.ds(ks, TK), :],
                            bias_ref[(t0 - ks) // HALF, 0])
                for r in range(RES):
                    st1[r][jj] = rows(stat, 8 * r, 8)

        qr, kr, side = TQ // D4_RES, TK // D4_RES, HALF // D4_RES
        st4 = [[None] * D4_RES for _ in range(RES)]
        for b in range(D4_RES):
            for j4 in range(TQ // qr):
                ml = base + qr * j4
                mg = m0 + qr * j4
                ks = pl.multiple_of(jnp.clip(mg - side, 0, M - kr), side)
                res4 = [D4_RES * a + b for a in range(RES // D4_RES)]
                q = jnp.concatenate([q_ref[bi, 0, r, ml:ml + qr, :] for r in res4], axis=0)
                k = jnp.concatenate([k16_ref[bi, 0, r, pl.ds(ks, kr), :] for r in res4], axis=0)
                v = jnp.concatenate([v16_ref[bi, 0, r, pl.ds(ks, kr), :] for r in res4], axis=0)
                old = cat([st1[r][(qr // 8) * j4 + c] for r in res4 for c in range(qr // 8)])
                stat = merge(old, tile(q, k, v, bias_ref[3 + (mg - ks) // side, 0]))
                for a, r in enumerate(res4):
                    st4[r][j4] = rows(stat, qr * a, qr)

        ks = pl.multiple_of(jnp.clip(m0 - HALF, 0, M - TK), HALF)
        for r in range(RES):
            new = tile(q_ref[bi, 0, r, base:base + TQ, :], k16_ref[bi, 0, r, pl.ds(ks, TK), :],
                       v16_ref[bi, 0, r, pl.ds(ks, TK), :], bias_ref[6 + (m0 - ks) // HALF, 0])
            _, l_new, acc_new = merge(cat(st4[r]), new)
            gate = ga_ref[bi, 0, r, base:base + TQ, :].astype(f32)
            o_ref[bi, 0, r, base:base + TQ, :] = (acc_new / l_new * gate).astype(bf16)


def _attention(q16, k16, v16, kn, vn, ga16, bias, headmask):
    B, _, _, M, _ = q16.shape
    S = M * RES
    nsub = min(MAX_SUBTILES, M // TQ)
    rows = nsub * TQ
    nbatch = math.gcd(B, MAX_SUBTILES // nsub)
    tile_spec = pl.BlockSpec((nbatch, 1, RES, rows, LANES), lambda j, b, t: (b, j, 0, t, 0))
    seq16_spec = pl.BlockSpec((nbatch, 1, RES, M, LANES), lambda j, b, t: (b, j, 0, 0, 0))
    nat_spec = pl.BlockSpec((nbatch, 1, S, LANES), lambda j, b, t: (b, j, 0, 0))
    return pl.pallas_call(
        functools.partial(_attn_kernel, M=M, S=S, nsub=nsub, nbatch=nbatch),
        grid=(N_PAIRS, B // nbatch, M // rows),
        in_specs=[tile_spec, seq16_spec, seq16_spec, nat_spec, nat_spec, tile_spec,
                  pl.BlockSpec((9, 1, 2 * TQ, TK), lambda j, b, t: (0, j, 0, 0)),
                  pl.BlockSpec((2, LANES), lambda j, b, t: (0, 0))],
        out_specs=tile_spec,
        out_shape=jax.ShapeDtypeStruct((B, N_PAIRS, RES, M, LANES), bf16),
        compiler_params=pltpu.CompilerParams(
            dimension_semantics=("parallel", "parallel", "arbitrary"), vmem_limit_bytes=VMEM_LIMIT),
        name="attn",
    )(q16, k16, v16, kn, vn, ga16, bias, headmask)


def _cmul_const(z, k, n):
    re, im = z
    if k == 0:
        return z
    if 4 * k == n:
        return -im, re
    c = math.cos(2.0 * math.pi * k / n)
    s = math.sin(2.0 * math.pi * k / n)
    return re * c - im * s, im * c + re * s


def _fft_list(xs):
    n = len(xs)
    if n == 1:
        return xs
    even = _fft_list(xs[0::2])
    odd = _fft_list(xs[1::2])
    out = [None] * n
    for k in range(n // 2):
        t = _cmul_const(odd[k], k, n)
        out[k] = (even[k][0] + t[0], even[k][1] + t[1])
        out[k + n // 2] = (even[k][0] - t[0], even[k][1] - t[1])
    return out


def _fnet_kernel(u_ref, g_ref, gf_ref, tw_ref, d_ref, o_ref, pq_ref, br_ref, bi_ref, *, M):
    def channel_body(i, carry):
        for c in range(DFT_UNROLL):
            rows = pl.ds(pl.multiple_of((i * DFT_UNROLL + c) * M, M), M)
            pq_ref[rows, :] = jnp.dot(u_ref[0, rows, :], g_ref[...], preferred_element_type=f32).astype(bf16)
        return carry

    lax.fori_loop(0, RES // DFT_UNROLL, channel_body, 0)

    rb = 16

    def butterfly_body(i, carry):
        row = pl.multiple_of(i * rb, rb)
        tw = tw_ref[pl.ds(row, rb), :]
        for half in range(D_FOURIER // LANES):
            lanes = slice(half * LANES, (half + 1) * LANES)
            zs = []
            for s2 in range(RES):
                rows = pl.ds(pl.multiple_of(s2 * M + row, rb), rb)
                zs.append((pq_ref[rows, lanes].astype(f32),
                           pq_ref[rows, slice(D_FOURIER + half * LANES, D_FOURIER + (half + 1) * LANES)].astype(f32)))
            ys = _fft_list(zs)
            for k2 in range(RES):
                re, im = ys[k2]
                if k2:
                    c = tw[:, k2:k2 + 1]
                    s = tw[:, RES + k2:RES + k2 + 1]
                    re, im = re * c - im * s, im * c + re * s
                br_ref[k2, pl.ds(row, rb), lanes] = re.astype(bf16)
                bi_ref[k2, pl.ds(row, rb), lanes] = im.astype(bf16)
        return carry

    lax.fori_loop(0, M // rb, butterfly_body, 0)

    def dft_body(i, carry):
        for u in range(DFT_UNROLL):
            k2 = i * DFT_UNROLL + u
            rhs = jnp.concatenate([br_ref[k2], bi_ref[k2]], axis=0)
            res = jnp.dot(d_ref[...], rhs, preferred_element_type=f32)
            o_ref[0, k2] = (res * gf_ref[0, k2].astype(f32)).astype(bf16)
        return carry

    lax.fori_loop(0, RES // DFT_UNROLL, dft_body, 0)


def _fourier(u, gmat, gf16, tw, dmat):
    B, S, _ = u.shape
    M = S // RES
    seq_spec = pl.BlockSpec((1, S, D_FOURIER), lambda b: (b, 0, 0))
    res_spec = pl.BlockSpec((1, RES, M, D_FOURIER), lambda b: (b, 0, 0, 0))
    return pl.pallas_call(
        functools.partial(_fnet_kernel, M=M),
        grid=(B,),
        in_specs=[seq_spec, pl.BlockSpec((D_FOURIER, 2 * D_FOURIER), lambda b: (0, 0)), res_spec,
                  pl.BlockSpec((M, LANES), lambda b: (0, 0)),
                  pl.BlockSpec((M, 2 * M), lambda b: (0, 0))],
        out_specs=res_spec,
        out_shape=jax.ShapeDtypeStruct((B, RES, M, D_FOURIER), bf16),
        scratch_shapes=[pltpu.VMEM((S, 2 * D_FOURIER), bf16)] + [pltpu.VMEM((RES, M, D_FOURIER), bf16)] * 2,
        compiler_params=pltpu.CompilerParams(
            dimension_semantics=("parallel",), vmem_limit_bytes=VMEM_LIMIT),
        name="fnet",
    )(u, gmat, gf16, tw, dmat)


def _outproj_kernel(a_hbm, f_ref, x_hbm, w_ref, y_ref, scr_ref, xbuf_ref, abuf_ref, sem_ref, *, tmm, steps, total):
    tm = RES * tmm
    t = pl.program_id(0) * steps + pl.program_id(1)

    def copies(step):
        slot = step % X_BUFFERS
        rows = pl.ds(pl.multiple_of((step % steps) * tm, tm), tm)
        mrows = pl.ds(pl.multiple_of((step % steps) * tmm, tmm), tmm)
        return (pltpu.make_async_copy(x_hbm.at[step // steps, rows, :], xbuf_ref.at[slot], sem_ref.at[0, slot]),
                pltpu.make_async_copy(a_hbm.at[step // steps, :, :, mrows, :], abuf_ref.at[slot],
                                      sem_ref.at[1, slot]))

    @pl.when(t == 0)
    def _():
        for k in range(X_BUFFERS - 1):
            for cp in copies(k):
                cp.start()

    @pl.when(t + X_BUFFERS - 1 < total)
    def _():
        for cp in copies(t + X_BUFFERS - 1):
            cp.start()

    for cp in copies(t):
        cp.wait()
    x_ref = xbuf_ref.at[t % X_BUFFERS]
    a_ref = abuf_ref.at[t % X_BUFFERS]
    attn = jnp.concatenate([a_ref[j].reshape(tm, LANES) for j in range(N_PAIRS)], axis=-1)
    mix = jnp.concatenate([attn, f_ref[0].reshape(tm, D_FOURIER)], axis=-1)
    y = jnp.dot(mix, w_ref[...], preferred_element_type=f32)
    for c in range(D_MODEL // LANES):
        for r in range(RES):
            scr_ref[c, pl.ds(r, tmm, stride=PITCH), :] = y[r * tmm:(r + 1) * tmm, c * LANES:(c + 1) * LANES]
    for c in range(D_MODEL // LANES):
        cols = slice(c * LANES, (c + 1) * LANES)
        for g in range(tmm):
            rows = slice(g * RES, (g + 1) * RES)
            y_ref[0, rows, cols] = x_ref[rows, cols] + scr_ref[c, g * PITCH:g * PITCH + RES, :]


def _outproj(attn16, four16, x, w_out, tmm=64):
    B, S, _ = x.shape
    M = S // RES
    tm = RES * tmm
    steps = M // tmm
    return pl.pallas_call(
        functools.partial(_outproj_kernel, tmm=tmm, steps=steps, total=B * steps),
        grid=(B, steps),
        in_specs=[pl.BlockSpec(memory_space=pl.ANY),
                  pl.BlockSpec((1, RES, tmm, D_FOURIER), lambda b, i: (b, 0, i, 0)),
                  pl.BlockSpec(memory_space=pl.ANY),
                  pl.BlockSpec((D_MODEL, D_MODEL), lambda b, i: (0, 0))],
        out_specs=pl.BlockSpec((1, tm, D_MODEL), lambda b, i: (b, i, 0)),
        out_shape=jax.ShapeDtypeStruct((B, S, D_MODEL), f32),
        scratch_shapes=[pltpu.VMEM((D_MODEL // LANES, tmm * PITCH, LANES), f32),
                        pltpu.VMEM((X_BUFFERS, tm, D_MODEL), f32),
                        pltpu.VMEM((X_BUFFERS, N_PAIRS, RES, tmm, LANES), bf16),
                        pltpu.SemaphoreType.DMA((2, X_BUFFERS))],
        compiler_params=pltpu.CompilerParams(
            dimension_semantics=("arbitrary", "arbitrary"), vmem_limit_bytes=VMEM_LIMIT),
        name="outproj",
    )(attn16, four16, x, w_out)


def _encoder_layer(x, norm_g, w_in, qg, kg, bd, bias, headmask, w_out, gmat):
    tw, dmat = _dft_tables(x.shape[1])
    q16, k16, v16, ga16, gf16, kn, vn, u = _inproj(x, norm_g, w_in, qg, kg, bd)
    attn16 = _attention(q16, k16, v16, kn, vn, ga16, bias, headmask)
    four16 = _fourier(u, gmat, gf16, jnp.asarray(tw), jnp.asarray(dmat, bf16))
    return _outproj(attn16, four16, x, w_out)


def _layer_params(norm_g, w_in, q_norm_g, k_norm_g, rel_bias, w_four, w_out):
    bias = _bias_tiles(rel_bias)
    headmask = jnp.asarray(np.stack([np.arange(LANES) < HEAD_DIM, np.arange(LANES) >= HEAD_DIM]), bf16)
    bd = jnp.asarray(np.kron(np.eye(MXU // HEAD_DIM), np.full((HEAD_DIM, HEAD_DIM), 1.0 / HEAD_DIM)), bf16)
    w_in_bf, w_out_bf, gmat = _prepare_weights(w_in, w_four, w_out)
    qg = jnp.tile(q_norm_g, N_HEADS)[None, :] * (LOG2E / math.sqrt(HEAD_DIM))
    kg = jnp.tile(k_norm_g, N_HEADS)[None, :]
    return (norm_g[None, :], w_in_bf, qg, kg, bd, bias, headmask, w_out_bf, gmat)


def kernel(x_prompt, x_sample, norm_g, w_in, q_norm_g, k_norm_g, rel_bias, w_four, w_out):
    y_prompt, y_sample = x_prompt, x_sample
    for layer in range(norm_g.shape[0]):
        args = _layer_params(norm_g[layer], w_in[layer], q_norm_g[layer], k_norm_g[layer], rel_bias,
                             w_four[layer], w_out[layer])
        y_prompt = _encoder_layer(y_prompt, *args)
        y_sample = _encoder_layer(y_sample, *args)
    return (y_prompt, y_sample)
```
